```python
import math
import jax, jax.numpy as jnp
from jax import lax
import numpy as np

D_MODEL = 1024
BATCH = 8
SEQ = 2048
DEPTH = 1
DEC_BATCH = 16
DEC_SEQ = 2048
PAST_LEN = 128

GRID_W = 64
SSM_WIDTH = D_MODEL // 2
SSM_GROUP_CH = 16
N_SSM_GROUPS = SSM_WIDTH // SSM_GROUP_CH
SSM_STATE = 64
ATTN_WIDTH = D_MODEL - SSM_WIDTH
HEAD_DIM = 64
N_HEADS_ATTN = ATTN_WIDTH // HEAD_DIM
MIX_WIDTH = SSM_WIDTH + ATTN_WIDTH
IN_PROJ_WIDTH = SSM_WIDTH + 3 * ATTN_WIDTH
NA_WIN_ROWS = 8
NA_WIN_COLS = 16
D_FF = -(-8 * D_MODEL // (3 * 256)) * 256
N_MOD = 6
EPS = 1e-6

kernel_name = "hymba_s5_natten_adaln_encoder"


def rms_norm(x, gain):
    xf = x.astype(jnp.float32)
    y = xf * lax.rsqrt(jnp.mean(xf * xf, axis=-1, keepdims=True) + EPS)
    return (y * gain.astype(jnp.float32)).astype(x.dtype)


def modulate(h, shift, scale):
    return h * (1 + scale[:, None, :]) + shift[:, None, :]


def _linear_recurrence_op(left, right):
    a_l, b_l = left
    a_r, b_r = right
    return a_r * a_l, a_r * b_l + b_r


def s5_bidirectional(u, a_re, a_im, log_dt, b_re, b_im, c_re, c_im, d_skip):
    bsz, seq_len, _ = u.shape
    f32 = jnp.float32
    uf = u.astype(f32).reshape(bsz, seq_len, N_SSM_GROUPS, SSM_GROUP_CH)
    uc = uf.astype(jnp.complex64)
    y = d_skip.astype(f32).reshape(N_SSM_GROUPS, SSM_GROUP_CH) * uf
    for direction in range(2):
        lam = lax.complex(jnp.minimum(a_re[direction].astype(f32), -1e-4),
                          a_im[direction].astype(f32))
        dt = jnp.exp(log_dt[direction].astype(f32))[:, None]
        a_bar = jnp.exp(lam * dt)
        b = lax.complex(b_re[direction].astype(f32), b_im[direction].astype(f32))
        b_bar = ((a_bar - 1.0) / lam)[:, :, None] * b
        bu = jnp.einsum("gph,blgh->blgp", b_bar, uc)
        a_seq = jnp.broadcast_to(a_bar, bu.shape)
        _, states = lax.associative_scan(_linear_recurrence_op, (a_seq, bu),
                                         reverse=(direction == 1), axis=1)
        c = lax.complex(c_re[direction].astype(f32), c_im[direction].astype(f32))
        y = y + jnp.einsum("ghp,blgp->blgh", c, states).real
    return y.reshape(bsz, seq_len, SSM_WIDTH).astype(u.dtype)


def neighbourhood_attention(q, k, v, rpb):
    bsz, seq_len, _ = q.shape
    rows = seq_len // GRID_W
    wh = min(NA_WIN_ROWS, rows)
    grid = (bsz, rows, GRID_W, N_HEADS_ATTN, HEAD_DIM)
    qg, kg, vg = q.reshape(grid), k.reshape(grid), v.reshape(grid)
    cols = jnp.arange(GRID_W)
    col_start = jnp.clip(cols - NA_WIN_COLS // 2, 0, GRID_W - NA_WIN_COLS)
    col_idx = col_start[:, None] + jnp.arange(NA_WIN_COLS)[None, :]
    dc = col_idx - cols[:, None]
    scale = HEAD_DIM ** -0.5

    def one_row(r):
        rs = jnp.clip(r - wh // 2, 0, rows - wh)
        q_r = lax.dynamic_index_in_dim(qg, r, axis=1, keepdims=False)
        k_blk = lax.dynamic_slice_in_dim(kg, rs, wh, axis=1)
        v_blk = lax.dynamic_slice_in_dim(vg, rs, wh, axis=1)
        k_win = k_blk[:, :, col_idx]
        v_win = v_blk[:, :, col_idx]
        dr = rs + jnp.arange(wh) - r
        bias = rpb[:, dr[None, :, None] + NA_WIN_ROWS - 1,
                   dc[:, None, :] + NA_WIN_COLS - 1]
        s = jnp.einsum("bchd,bicjhd->bhcij", q_r, k_win).astype(jnp.float32) * scale
        s = s + bias.astype(jnp.float32)[None]
        p = jax.nn.softmax(s.reshape(s.shape[:3] + (wh * NA_WIN_COLS,)), axis=-1)
        p = p.reshape(s.shape).astype(v.dtype)
        return jnp.einsum("bhcij,bicjhd->bchd", p, v_win)

    out = lax.map(one_row, jnp.arange(rows))
    return jnp.transpose(out, (1, 0, 2, 3, 4)).reshape(bsz, seq_len, ATTN_WIDTH)


def encoder_trunk(x, c, w_ada, b_ada, norm_mix, w_in, ssm_a_re, ssm_a_im, ssm_log_dt,
                  ssm_b_re, ssm_b_im, ssm_c_re, ssm_c_im, ssm_d, w_glu, b_glu,
                  norm_ssm_out, na_rpb, norm_attn_out, w_out, norm_ffn, w_ffn_gate,
                  w_ffn_up, w_ffn_down, norm_final):
    for layer in range(DEPTH):
        mod = jax.nn.silu(c) @ w_ada[layer] + b_ada[layer]
        sh_mix, sc_mix, g_mix, sh_ffn, sc_ffn, g_ffn = jnp.split(mod, N_MOD, axis=-1)
        h = modulate(rms_norm(x, norm_mix[layer]), sh_mix, sc_mix)
        proj = h @ w_in[layer]
        u, q, k, v = jnp.split(proj, [SSM_WIDTH, SSM_WIDTH + ATTN_WIDTH,
                                      SSM_WIDTH + 2 * ATTN_WIDTH], axis=-1)
        y_ssm = s5_bidirectional(u, ssm_a_re[layer], ssm_a_im[layer], ssm_log_dt[layer],
                                 ssm_b_re[layer], ssm_b_im[layer], ssm_c_re[layer],
                                 ssm_c_im[layer], ssm_d[layer])
        y_ssm = jax.nn.gelu(y_ssm)
        y_ssm = y_ssm * jax.nn.sigmoid(y_ssm @ w_glu[layer] + b_glu[layer])
        y_att = neighbourhood_attention(q, k, v, na_rpb[layer])
        mixed = jnp.concatenate([rms_norm(y_ssm, norm_ssm_out[layer]),
                                 rms_norm(y_att, norm_attn_out[layer])], axis=-1)
        x = x + g_mix[:, None, :] * (mixed @ w_out[layer])
        h = modulate(rms_norm(x, norm_ffn[layer]), sh_ffn, sc_ffn)
        f = (jax.nn.silu(h @ w_ffn_gate[layer]) * (h @ w_ffn_up[layer])) @ w_ffn_down[layer]
        x = x + g_ffn[:, None, :] * f
    return rms_norm(x, norm_final)


def setup_inputs(seed: int = 0) -> dict:
    key = jax.random.key(seed)
    ks = jax.random.split(key, 32)
    f32 = jnp.float32
    nrm = lambda k, shape, s: jax.random.normal(k, shape, f32) * s
    gain = lambda k, shape: 1.0 + 0.01 * jax.random.normal(k, shape, f32)
    L2 = (DEPTH, 2)
    G, P, H = N_SSM_GROUPS, SSM_STATE, SSM_GROUP_CH
    a_im_init = math.pi * jnp.arange(P, dtype=f32)
    return {
        "x_prompt": nrm(ks[0], (BATCH, SEQ, D_MODEL), 1.0),
        "x_sample": nrm(ks[1], (DEC_BATCH, DEC_SEQ, D_MODEL), 1.0),
        "c_prompt": nrm(ks[2], (BATCH, D_MODEL), 1.0),
        "c_sample": nrm(ks[3], (DEC_BATCH, D_MODEL), 1.0),
        "w_ada": nrm(ks[4], (DEPTH, D_MODEL, N_MOD * D_MODEL), 0.5 * D_MODEL ** -0.5),
        "b_ada": nrm(ks[5], (DEPTH, N_MOD * D_MODEL), 0.01),
        "norm_mix": gain(ks[6], (DEPTH, D_MODEL)),
        "w_in": nrm(ks[7], (DEPTH, D_MODEL, IN_PROJ_WIDTH), D_MODEL ** -0.5),
        "ssm_a_re": -0.5 + nrm(ks[8], L2 + (G, P), 0.01),
        "ssm_a_im": a_im_init + nrm(ks[9], L2 + (G, P), 0.01),
        "ssm_log_dt": jax.random.uniform(ks[10], L2 + (G,), f32,
                                         math.log(1e-3), math.log(1e-1)),
        "ssm_b_re": nrm(ks[11], L2 + (G, P, H), (2 * H) ** -0.5),
        "ssm_b_im": nrm(ks[12], L2 + (G, P, H), (2 * H) ** -0.5),
        "ssm_c_re": nrm(ks[13], L2 + (G, H, P), (2 * P) ** -0.5),
        "ssm_c_im": nrm(ks[14], L2 + (G, H, P), (2 * P) ** -0.5),
        "ssm_d": nrm(ks[15], (DEPTH, SSM_WIDTH), 1.0),
        "w_glu": nrm(ks[16], (DEPTH, SSM_WIDTH, SSM_WIDTH), SSM_WIDTH ** -0.5),
        "b_glu": nrm(ks[17], (DEPTH, SSM_WIDTH), 0.01),
        "norm_ssm_out": gain(ks[18], (DEPTH, SSM_WIDTH)),
        "na_rpb": nrm(ks[19], (DEPTH, N_HEADS_ATTN, 2 * NA_WIN_ROWS - 1, 2 * NA_WIN_COLS - 1), 0.02),
        "norm_attn_out": gain(ks[20], (DEPTH, ATTN_WIDTH)),
        "w_out": nrm(ks[21], (DEPTH, MIX_WIDTH, D_MODEL), MIX_WIDTH ** -0.5),
        "norm_ffn": gain(ks[22], (DEPTH, D_MODEL)),
        "w_ffn_gate": nrm(ks[23], (DEPTH, D_MODEL, D_FF), D_MODEL ** -0.5),
        "w_ffn_up": nrm(ks[24], (DEPTH, D_MODEL, D_FF), D_MODEL ** -0.5),
        "w_ffn_down": nrm(ks[25], (DEPTH, D_FF, D_MODEL), D_FF ** -0.5),
        "norm_final": gain(ks[26], (D_MODEL,)),
    }


def reference(x_prompt, x_sample, c_prompt, c_sample, w_ada, b_ada, norm_mix, w_in,
              ssm_a_re, ssm_a_im, ssm_log_dt, ssm_b_re, ssm_b_im, ssm_c_re, ssm_c_im,
              ssm_d, w_glu, b_glu, norm_ssm_out, na_rpb, norm_attn_out, w_out,
              norm_ffn, w_ffn_gate, w_ffn_up, w_ffn_down, norm_final):
    y_prompt = encoder_trunk(x_prompt, c_prompt, w_ada, b_ada, norm_mix, w_in, ssm_a_re,
                             ssm_a_im, ssm_log_dt, ssm_b_re, ssm_b_im, ssm_c_re, ssm_c_im,
                             ssm_d, w_glu, b_glu, norm_ssm_out, na_rpb, norm_attn_out,
                             w_out, norm_ffn, w_ffn_gate, w_ffn_up, w_ffn_down, norm_final)
    y_sample = encoder_trunk(x_sample, c_sample, w_ada, b_ada, norm_mix, w_in, ssm_a_re,
                             ssm_a_im, ssm_log_dt, ssm_b_re, ssm_b_im, ssm_c_re, ssm_c_im,
                             ssm_d, w_glu, b_glu, norm_ssm_out, na_rpb, norm_attn_out,
                             w_out, norm_ffn, w_ffn_gate, w_ffn_up, w_ffn_down, norm_final)
    return (y_prompt, y_sample)
```

```python
import functools
import math

import jax
import jax.numpy as jnp
from jax import lax
from jax.experimental import pallas as pl
from jax.experimental.pallas import tpu as pltpu

D_MODEL = 1024
GRID_W = 64
SSM_WIDTH = D_MODEL // 2
SSM_GROUP_CH = 16
N_SSM_GROUPS = SSM_WIDTH // SSM_GROUP_CH
SSM_STATE = 64
ATTN_WIDTH = D_MODEL - SSM_WIDTH
HEAD_DIM = 64
N_HEADS_ATTN = ATTN_WIDTH // HEAD_DIM
IN_PROJ_WIDTH = SSM_WIDTH + 3 * ATTN_WIDTH
NA_WIN_ROWS = 8
NA_WIN_COLS = 16
D_FF = -(-8 * D_MODEL // (3 * 256)) * 256
N_MOD = 6
EPS = 1e-6

LANES = 128
SSM_CHUNK = 16
CHUNK_COLS = SSM_CHUNK * SSM_GROUP_CH
HEADS_PER_BLOCK = LANES // HEAD_DIM
N_HEAD_BLOCKS = N_HEADS_ATTN // HEADS_PER_BLOCK
FF_CHUNK = 256
MASK_VALUE = -1e30
VMEM_LIMIT = 56 * 1024 * 1024

F32 = jnp.float32
BF16 = jnp.bfloat16


def _rms(x, gain):
    return x * lax.rsqrt(jnp.mean(x * x, axis=-1, keepdims=True) + EPS) * gain


def _ada_kernel(c_ref, w_ref, b_ref, o_ref):
    c = c_ref[...]
    s = c * jax.nn.sigmoid(c)
    o_ref[...] = jnp.dot(s, w_ref[...], preferred_element_type=F32,
                         precision=lax.Precision.HIGHEST) + b_ref[...]


def _ada_mod(c, w_ada, b_ada):
    nb = c.shape[0]
    n_out = w_ada.shape[1]
    tn = 1536
    return pl.pallas_call(
        _ada_kernel,
        grid=(n_out // tn,),
        in_specs=[pl.BlockSpec((nb, D_MODEL), lambda j: (0, 0)),
                  pl.BlockSpec((D_MODEL, tn), lambda j: (0, j)),
                  pl.BlockSpec((1, tn), lambda j: (0, j))],
        out_specs=pl.BlockSpec((nb, tn), lambda j: (0, j)),
        out_shape=jax.ShapeDtypeStruct((nb, n_out), F32),
        name="ada_mod",
    )(c, w_ada, b_ada.reshape(1, n_out))


def _inproj_kernel(x_ref, sh_ref, sc_ref, g_ref, w_ref, u_ref, qkv_ref):
    x = x_ref[0]
    h = _rms(x, g_ref[...]) * (1.0 + sc_ref[0]) + sh_ref[0]
    hb = h.astype(BF16)
    u_ref[0] = jnp.dot(hb, w_ref[:, :SSM_WIDTH], preferred_element_type=F32)
    for j in range(3):
        lo = SSM_WIDTH + j * ATTN_WIDTH
        r = jnp.dot(hb, w_ref[:, lo:lo + ATTN_WIDTH], preferred_element_type=F32)
        if j == 0:
            r = r * (HEAD_DIM ** -0.5)
        qkv_ref[0, :, j * ATTN_WIDTH:(j + 1) * ATTN_WIDTH] = r.astype(BF16)


def _in_proj(x, shift, scale, gain, w_in_bf16, tm=512):
    nb, seq, _ = x.shape
    vec = pl.BlockSpec((1, 1, D_MODEL), lambda b, i: (b, 0, 0))
    return pl.pallas_call(
        _inproj_kernel,
        grid=(nb, seq // tm),
        in_specs=[pl.BlockSpec((1, tm, D_MODEL), lambda b, i: (b, i, 0)),
                  vec, vec,
                  pl.BlockSpec((1, D_MODEL), lambda b, i: (0, 0)),
                  pl.BlockSpec((D_MODEL, IN_PROJ_WIDTH), lambda b, i: (0, 0))],
        out_specs=[pl.BlockSpec((1, tm, SSM_WIDTH), lambda b, i: (b, i, 0)),
                   pl.BlockSpec((1, tm, 3 * ATTN_WIDTH), lambda b, i: (b, i, 0))],
        out_shape=[jax.ShapeDtypeStruct((nb, seq, SSM_WIDTH), F32),
                   jax.ShapeDtypeStruct((nb, seq, 3 * ATTN_WIDTH), BF16)],
        compiler_params=pltpu.CompilerParams(vmem_limit_bytes=VMEM_LIMIT),
        name="in_proj",
    )(x, shift, scale, gain, w_in_bf16)


def _ssm_matrices(a_re, a_im, log_dt, b_re, b_im, c_re, c_im, d_skip):
    hp = lax.Precision.HIGHEST
    c_len, g_n, p_n, h_n = SSM_CHUNK, N_SSM_GROUPS, SSM_STATE, SSM_GROUP_CH
    lam = lax.complex(jnp.minimum(a_re.astype(F32), -1e-4), a_im.astype(F32))
    dt = jnp.exp(log_dt.astype(F32))[:, :, None]
    a_bar = jnp.exp(lam * dt)
    b = lax.complex(b_re.astype(F32), b_im.astype(F32))
    b_bar = ((a_bar - 1.0) / lam)[..., None] * b
    c = lax.complex(c_re.astype(F32), c_im.astype(F32))
    j = jnp.arange(c_len + 1, dtype=F32)[None, :, None, None]
    apow = jnp.exp((lam * dt)[:, None] * j)

    kern = jnp.einsum("dghp,djgp,dgpi->djghi", c, apow[:, :c_len], b_bar, precision=hp).real
    s_idx = jnp.arange(c_len)[:, None]
    t_idx = jnp.arange(c_len)[None, :]
    lag_f = jnp.clip(t_idx - s_idx, 0, c_len - 1)
    lag_b = jnp.clip(s_idx - t_idx, 0, c_len - 1)
    t_f = jnp.where((t_idx >= s_idx)[:, :, None, None, None], kern[0][lag_f], 0.0)
    t_b = jnp.where((s_idx >= t_idx)[:, :, None, None, None], kern[1][lag_b], 0.0)
    eye_t = jnp.eye(c_len, dtype=F32)[:, :, None, None, None]
    eye_h = jnp.eye(h_n, dtype=F32)[None, None, None]
    d_gh = d_skip.astype(F32).reshape(g_n, h_n)
    t_d = eye_t * eye_h * d_gh[None, None, :, :, None]
    toep = jnp.transpose(t_f + t_b + t_d, (2, 0, 4, 1, 3)).reshape(g_n, CHUNK_COLS, CHUNK_COLS)

    e_f = apow[0, c_len - 1 - jnp.arange(c_len)][:, :, :, None] * b_bar[0][None]
    e_b = apow[1, jnp.arange(c_len)][:, :, :, None] * b_bar[1][None]
    bm = jnp.stack([e_f.real, e_b.real, e_f.imag, e_b.imag], axis=0)
    bm = jnp.transpose(bm, (2, 1, 4, 0, 3)).reshape(g_n, CHUNK_COLS, 4 * p_n)

    w_f = c[0][None] * apow[0, 1:c_len + 1][:, :, None, :]
    w_b = c[1][None] * apow[1, c_len - jnp.arange(c_len)][:, :, None, :]
    cm = jnp.stack([w_f.real, w_b.real, -w_f.imag, -w_b.imag], axis=0)
    cm = jnp.transpose(cm, (2, 0, 4, 1, 3)).reshape(g_n, 4 * p_n, CHUNK_COLS)

    a16 = apow[:, c_len]
    a16_re = jnp.concatenate([a16[0].real, a16[1].real], axis=-1)[:, None, :]
    a16_im = jnp.concatenate([a16[0].imag, a16[1].imag], axis=-1)[:, None, :]
    tcm = jnp.concatenate([toep, cm], axis=1)
    return bm.astype(BF16), tcm.astype(BF16), a16_re, a16_im


def _ssm_kernel(u_ref, bm_ref, tcm_ref, are_ref, aim_ref, y_ref, s_ref, x_ref, *, nb, n_chunks):
    ub = u_ref[0].astype(BF16)
    s_ref[...] = jnp.dot(ub, bm_ref[0], preferred_element_type=F32)
    a_re = jnp.broadcast_to(are_ref[0], (nb, LANES))
    a_im = jnp.broadcast_to(aim_ref[0], (nb, LANES))
    is_fwd = lax.broadcasted_iota(jnp.int32, (nb, LANES), 1) < SSM_STATE
    half = SSM_STATE

    def step(i, carry):
        x_re, x_im = carry
        rf = pl.ds(pl.multiple_of(i * nb, nb), nb)
        rb = pl.ds(pl.multiple_of((n_chunks - 1 - i) * nb, nb), nb)
        x_ref[rf, 0:half] = x_re[:, 0:half]
        x_ref[rf, LANES:LANES + half] = x_im[:, 0:half]
        x_ref[rb, half:LANES] = x_re[:, half:LANES]
        x_ref[rb, LANES + half:2 * LANES] = x_im[:, half:LANES]
        s_re = jnp.where(is_fwd, s_ref[rf, 0:LANES], s_ref[rb, 0:LANES])
        s_im = jnp.where(is_fwd, s_ref[rf, LANES:2 * LANES], s_ref[rb, LANES:2 * LANES])
        n_re = a_re * x_re - a_im * x_im + s_re
        n_im = a_re * x_im + a_im * x_re + s_im
        return n_re, n_im

    zero = jnp.zeros((nb, LANES), F32)
    lax.fori_loop(0, n_chunks, step, (zero, zero))
    y = jnp.dot(ub, tcm_ref[0, :CHUNK_COLS, :], preferred_element_type=F32)
    y = y + jnp.dot(x_ref[...].astype(BF16), tcm_ref[0, CHUNK_COLS:, :], preferred_element_type=F32)
    y_ref[0] = y


def _ssm(u_g, bm, tcm, a16_re, a16_im, nb):
    g_n, rows, _ = u_g.shape
    n_chunks = rows // nb
    per_g = lambda g: (g, 0, 0)
    return pl.pallas_call(
        functools.partial(_ssm_kernel, nb=nb, n_chunks=n_chunks),
        grid=(g_n,),
        in_specs=[pl.BlockSpec((1, rows, CHUNK_COLS), per_g),
                  pl.BlockSpec((1, CHUNK_COLS, 4 * SSM_STATE), per_g),
                  pl.BlockSpec((1, CHUNK_COLS + 4 * SSM_STATE, CHUNK_COLS), per_g),
                  pl.BlockSpec((1, 1, LANES), per_g),
                  pl.BlockSpec((1, 1, LANES), per_g)],
        out_specs=pl.BlockSpec((1, rows, CHUNK_COLS), per_g),
        out_shape=jax.ShapeDtypeStruct((g_n, rows, CHUNK_COLS), F32),
        scratch_shapes=[pltpu.VMEM((rows, 4 * SSM_STATE), F32),
                        pltpu.VMEM((rows, 4 * SSM_STATE), F32)],
        compiler_params=pltpu.CompilerParams(vmem_limit_bytes=VMEM_LIMIT),
        name="ssm_chunked",
    )(u_g, bm, tcm, a16_re, a16_im)


def _attn_bias_table(rpb):
    wh = NA_WIN_ROWS
    cols = jnp.arange(GRID_W)
    col_start = jnp.clip(cols - NA_WIN_COLS // 2, 0, GRID_W - NA_WIN_COLS)
    kc = jnp.arange(GRID_W)
    valid = (kc[None, :] >= col_start[:, None]) & (kc[None, :] < col_start[:, None] + NA_WIN_COLS)
    dc = jnp.clip(kc[None, :] - cols[:, None] + NA_WIN_COLS - 1, 0, 2 * NA_WIN_COLS - 2)
    case = jnp.arange(wh)
    i = jnp.arange(wh)
    dr = i[None, :] - case[:, None] + NA_WIN_ROWS - 1
    tab = rpb.astype(F32)[:, dr[:, None, :, None], dc[None, :, None, :]]
    tab = jnp.where(valid[None, None, :, None, :], tab, MASK_VALUE)
    tab = tab.reshape(N_HEAD_BLOCKS, HEADS_PER_BLOCK, wh, GRID_W, wh * GRID_W)
    return jnp.transpose(tab, (0, 2, 1, 3, 4)).reshape(N_HEAD_BLOCKS, wh, HEADS_PER_BLOCK * GRID_W, wh * GRID_W)


def _attn_kernel(q_ref, k_ref, v_ref, bias_ref, o_ref, *, rows_per_step, n_rows):
    wh = min(NA_WIN_ROWS, n_rows)
    lane = lax.broadcasted_iota(jnp.int32, (GRID_W, LANES), 1)
    first_head = lane < HEAD_DIM
    zero = jnp.zeros((GRID_W, LANES), BF16)
    for j in range(rows_per_step):
        r = pl.program_id(2) * rows_per_step + j
        rs = jnp.clip(r - wh // 2, 0, n_rows - wh)
        q = q_ref[0, pl.ds(pl.multiple_of(r * GRID_W, GRID_W), GRID_W), :]
        qs = jnp.concatenate([jnp.where(first_head, q, zero), jnp.where(first_head, zero, q)], axis=0)
        kv_rows = pl.ds(pl.multiple_of(rs * GRID_W, GRID_W), wh * GRID_W)
        kb = k_ref[0, kv_rows, :]
        vb = v_ref[0, kv_rows, :]
        s = lax.dot_general(qs, kb, (((1,), (1,)), ((), ())), preferred_element_type=F32)
        s = s + bias_ref[0, r - rs]
        m = jnp.max(s, axis=-1, keepdims=True)
        p = jnp.exp(s - m)
        l = jnp.sum(p, axis=-1, keepdims=True)
        o = jnp.dot(p.astype(BF16), vb, preferred_element_type=F32) / l
        o_ref[0, j * GRID_W:(j + 1) * GRID_W, :] = jnp.where(first_head, o[:GRID_W], o[GRID_W:])


def _attention(qkv, bias_tab, rows_per_step=4):
    nb, seq, _ = qkv.shape
    n_rows = seq // GRID_W
    wh = min(NA_WIN_ROWS, n_rows)
    assert wh == NA_WIN_ROWS and n_rows % rows_per_step == 0
    nhb = N_HEAD_BLOCKS
    return pl.pallas_call(
        functools.partial(_attn_kernel, rows_per_step=rows_per_step, n_rows=n_rows),
        grid=(nhb, nb, n_rows // rows_per_step),
        in_specs=[pl.BlockSpec((1, seq, LANES), lambda h, b, r: (b, 0, h)),
                  pl.BlockSpec((1, seq, LANES), lambda h, b, r: (b, 0, nhb + h)),
                  pl.BlockSpec((1, seq, LANES), lambda h, b, r: (b, 0, 2 * nhb + h)),
                  pl.BlockSpec((1, wh, HEADS_PER_BLOCK * GRID_W, wh * GRID_W), lambda h, b, r: (h, 0, 0, 0))],
        out_specs=pl.BlockSpec((1, rows_per_step * GRID_W, LANES), lambda h, b, r: (b, r, h)),
        out_shape=jax.ShapeDtypeStruct((nb, seq, ATTN_WIDTH), F32),
        compiler_params=pltpu.CompilerParams(vmem_limit_bytes=VMEM_LIMIT),
        name="nbr_attention",
    )(qkv, qkv, qkv, bias_tab)


def _gelu_tanh(x):
    return 0.5 * x * (1.0 + jnp.tanh(math.sqrt(2.0 / math.pi) * (x + 0.044715 * (x * x * x))))


def _mix_kernel(x_ref, ys_ref, ya_ref, g_ref, wglu_ref, bglu_ref, ns_ref, na_ref, wout_ref, o_ref):
    ys = _gelu_tanh(ys_ref[0])
    z = jnp.dot(ys.astype(BF16), wglu_ref[...], preferred_element_type=F32) + bglu_ref[...]
    ys = ys * jax.nn.sigmoid(z)
    m_s = _rms(ys, ns_ref[...]).astype(BF16)
    m_a = _rms(ya_ref[0], na_ref[...]).astype(BF16)
    o = jnp.dot(m_s, wout_ref[:SSM_WIDTH, :], preferred_element_type=F32)
    o = o + jnp.dot(m_a, wout_ref[SSM_WIDTH:, :], preferred_element_type=F32)
    o_ref[0] = x_ref[0] + g_ref[0] * o


def _mix_out(x, y_ssm, y_att, g_mix, w_glu, b_glu, n_ssm, n_att, w_out, tm=512):
    nb, seq, _ = x.shape
    tok = lambda w: pl.BlockSpec((1, tm, w), lambda b, i: (b, i, 0))
    const = lambda shape: pl.BlockSpec(shape, lambda b, i: (0,) * len(shape))
    return pl.pallas_call(
        _mix_kernel,
        grid=(nb, seq // tm),
        in_specs=[tok(D_MODEL), tok(SSM_WIDTH), tok(ATTN_WIDTH),
                  pl.BlockSpec((1, 1, D_MODEL), lambda b, i: (b, 0, 0)),
                  const((SSM_WIDTH, SSM_WIDTH)), const((1, SSM_WIDTH)),
                  const((1, SSM_WIDTH)), const((1, ATTN_WIDTH)),
                  const((D_MODEL, D_MODEL))],
        out_specs=tok(D_MODEL),
        out_shape=jax.ShapeDtypeStruct((nb, seq, D_MODEL), F32),
        compiler_params=pltpu.CompilerParams(vmem_limit_bytes=VMEM_LIMIT),
        name="mix_out",
    )(x, y_ssm, y_att, g_mix, w_glu, b_glu, n_ssm, n_att, w_out)


def _ffn_kernel(x_ref, sh_ref, sc_ref, g_ref, nf_ref, wg_ref, wu_ref, wd_ref, nfin_ref, o_ref):
    x = x_ref[0]
    h = (_rms(x, nf_ref[...]) * (1.0 + sc_ref[0]) + sh_ref[0]).astype(BF16)
    f = jnp.zeros(x.shape, F32)
    for j in range(D_FF // FF_CHUNK):
        cols = slice(j * FF_CHUNK, (j + 1) * FF_CHUNK)
        gate = jnp.dot(h, wg_ref[:, cols], preferred_element_type=F32)
        up = jnp.dot(h, wu_ref[:, cols], preferred_element_type=F32)
        act = (gate * jax.nn.sigmoid(gate) * up).astype(BF16)
        f = f + jnp.dot(act, wd_ref[cols, :], preferred_element_type=F32)
    o_ref[0] = _rms(x + g_ref[0] * f, nfin_ref[...])


def _ffn(x, shift, scale, gate, n_ffn, w_gate, w_up, w_down, n_final, tm=512):
    nb, seq, _ = x.shape
    tok = pl.BlockSpec((1, tm, D_MODEL), lambda b, i: (b, i, 0))
    vec = pl.BlockSpec((1, 1, D_MODEL), lambda b, i: (b, 0, 0))
    const = lambda shape: pl.BlockSpec(shape, lambda b, i: (0,) * len(shape),
                                       pipeline_mode=pl.Buffered(1))
    return pl.pallas_call(
        _ffn_kernel,
        grid=(nb, seq // tm),
        in_specs=[tok, vec, vec, vec, const((1, D_MODEL)),
                  const((D_MODEL, D_FF)), const((D_MODEL, D_FF)), const((D_FF, D_MODEL)),
                  const((1, D_MODEL))],
        out_specs=tok,
        out_shape=jax.ShapeDtypeStruct((nb, seq, D_MODEL), F32),
        compiler_params=pltpu.CompilerParams(vmem_limit_bytes=VMEM_LIMIT),
        name="ffn_final",
    )(x, shift, scale, gate, n_ffn, w_gate, w_up, w_down, n_final)


def _to_group_layout(u):
    nb, seq, _ = u.shape
    n_chunks = seq // SSM_CHUNK
    u5 = u.reshape(nb, n_chunks, SSM_CHUNK, N_SSM_GROUPS, SSM_GROUP_CH)
    return jnp.transpose(u5, (3, 1, 0, 2, 4)).reshape(N_SSM_GROUPS, n_chunks * nb, CHUNK_COLS)


def _from_group_layout(y_g, nb):
    n_chunks = y_g.shape[1] // nb
    y5 = y_g.reshape(N_SSM_GROUPS, n_chunks, nb, SSM_CHUNK, SSM_GROUP_CH)
    return jnp.transpose(y5, (2, 1, 3, 0, 4)).reshape(nb, n_chunks * SSM_CHUNK, SSM_WIDTH)


def _trunk(x, mod, p):
    nb = x.shape[0]
    sh_mix, sc_mix, g_mix, sh_ffn, sc_ffn, g_ffn = [
        mod[:, i * D_MODEL:(i + 1) * D_MODEL].reshape(nb, 1, D_MODEL) for i in range(N_MOD)]
    u, qkv = _in_proj(x, sh_mix, sc_mix, p["norm_mix"], p["w_in"])
    y_g = _ssm(_to_group_layout(u), p["bm"], p["tcm"], p["a16_re"], p["a16_im"], nb)
    y_ssm = _from_group_layout(y_g, nb)
    y_att = _attention(qkv, p["bias_tab"])
    x1 = _mix_out(x, y_ssm, y_att, g_mix, p["w_glu"], p["b_glu"], p["norm_ssm_out"],
                  p["norm_attn_out"], p["w_out"])
    return _ffn(x1, sh_ffn, sc_ffn, g_ffn, p["norm_ffn"], p["w_ffn_gate"], p["w_ffn_up"],
                p["w_ffn_down"], p["norm_final"])


def kernel(x_prompt, x_sample, c_prompt, c_sample, w_ada, b_ada, norm_mix, w_in, ssm_a_re, ssm_a_im, ssm_log_dt, ssm_b_re, ssm_b_im, ssm_c_re, ssm_c_im, ssm_d, w_glu, b_glu, norm_ssm_out, na_rpb, norm_attn_out, w_out, norm_ffn, w_ffn_gate, w_ffn_up, w_ffn_down, norm_final):
    assert w_ada.shape[0] == 1, "single-layer trunk"
    row = lambda v: v.reshape(1, -1).astype(F32)
    bm, tcm, a16_re, a16_im = _ssm_matrices(ssm_a_re[0], ssm_a_im[0], ssm_log_dt[0], ssm_b_re[0],
                                            ssm_b_im[0], ssm_c_re[0], ssm_c_im[0], ssm_d[0])
    p = dict(
        norm_mix=row(norm_mix[0]), w_in=w_in[0].astype(BF16),
        bm=bm, tcm=tcm, a16_re=a16_re, a16_im=a16_im,
        bias_tab=_attn_bias_table(na_rpb[0]),
        w_glu=w_glu[0].astype(BF16), b_glu=row(b_glu[0]),
        norm_ssm_out=row(norm_ssm_out[0]), norm_attn_out=row(norm_attn_out[0]),
        w_out=w_out[0].astype(BF16), norm_ffn=row(norm_ffn[0]),
        w_ffn_gate=w_ffn_gate[0].astype(BF16), w_ffn_up=w_ffn_up[0].astype(BF16),
        w_ffn_down=w_ffn_down[0].astype(BF16), norm_final=row(norm_final),
    )
    n_prompt = x_prompt.shape[0]
    mod = _ada_mod(jnp.concatenate([c_prompt, c_sample], axis=0), w_ada[0], b_ada[0])
    y_prompt = _trunk(x_prompt, mod[:n_prompt], p)
    y_sample = _trunk(x_sample, mod[n_prompt:], p)
    return (y_prompt, y_sample)
```

```python
import functools
import math

import jax
import jax.numpy as jnp
from jax import lax
from jax.experimental import pallas as pl
from jax.experimental.pallas import tpu as pltpu

D_MODEL = 1024
GRID_W = 64
SSM_WIDTH = D_MODEL // 2
SSM_GROUP_CH = 16
N_SSM_GROUPS = SSM_WIDTH // SSM_GROUP_CH
SSM_STATE = 64
ATTN_WIDTH = D_MODEL - SSM_WIDTH
HEAD_DIM = 64
N_HEADS_ATTN = ATTN_WIDTH // HEAD_DIM
IN_PROJ_WIDTH = SSM_WIDTH + 3 * ATTN_WIDTH
NA_WIN_ROWS = 8
NA_WIN_COLS = 16
D_FF = -(-8 * D_MODEL // (3 * 256)) * 256
N_MOD = 6
EPS = 1e-6

LANES = 128
SSM_CHUNK = 16
CHUNK_COLS = SSM_CHUNK * SSM_GROUP_CH
HEADS_PER_BLOCK = LANES // HEAD_DIM
N_HEAD_BLOCKS = N_HEADS_ATTN // HEADS_PER_BLOCK
FF_CHUNK = 256
MASK_VALUE = -1e30
VMEM_LIMIT = 56 * 1024 * 1024

F32 = jnp.float32
BF16 = jnp.bfloat16


def _rms(x, gain):
    return x * lax.rsqrt(jnp.mean(x * x, axis=-1, keepdims=True) + EPS) * gain


def _ada_kernel(c_ref, w_ref, b_ref, o_ref):
    c = c_ref[...]
    s = c * jax.nn.sigmoid(c)
    o_ref[...] = jnp.dot(s, w_ref[...], preferred_element_type=F32,
                         precision=lax.Precision.HIGHEST) + b_ref[...]


def _ada_mod(c, w_ada, b_ada):
    nb = c.shape[0]
    n_out = w_ada.shape[1]
    tn = 1536
    return pl.pallas_call(
        _ada_kernel,
        grid=(n_out // tn,),
        in_specs=[pl.BlockSpec((nb, D_MODEL), lambda j: (0, 0)),
                  pl.BlockSpec((D_MODEL, tn), lambda j: (0, j)),
                  pl.BlockSpec((1, tn), lambda j: (0, j))],
        out_specs=pl.BlockSpec((nb, tn), lambda j: (0, j)),
        out_shape=jax.ShapeDtypeStruct((nb, n_out), F32),
        name="ada_mod",
    )(c, w_ada, b_ada.reshape(1, n_out))


def _inproj_kernel(x_ref, sh_ref, sc_ref, g_ref, w_ref, u_ref, qkv_ref):
    x = x_ref[0]
    h = _rms(x, g_ref[...]) * (1.0 + sc_ref[0]) + sh_ref[0]
    hb = h.astype(BF16)
    u_ref[0] = jnp.dot(hb, w_ref[:, :SSM_WIDTH], preferred_element_type=F32)
    for j in range(3):
        lo = SSM_WIDTH + j * ATTN_WIDTH
        r = jnp.dot(hb, w_ref[:, lo:lo + ATTN_WIDTH], preferred_element_type=F32)
        if j == 0:
            r = r * (HEAD_DIM ** -0.5)
        qkv_ref[0, :, j * ATTN_WIDTH:(j + 1) * ATTN_WIDTH] = r.astype(BF16)


def _in_proj(x, shift, scale, gain, w_in_bf16, tm=512):
    nb, seq, _ = x.shape
    vec = pl.BlockSpec((1, 1, D_MODEL), lambda b, i: (b, 0, 0))
    return pl.pallas_call(
        _inproj_kernel,
        grid=(nb, seq // tm),
        in_specs=[pl.BlockSpec((1, tm, D_MODEL), lambda b, i: (b, i, 0)),
                  vec, vec,
                  pl.BlockSpec((1, D_MODEL), lambda b, i: (0, 0)),
                  pl.BlockSpec((D_MODEL, IN_PROJ_WIDTH), lambda b, i: (0, 0))],
        out_specs=[pl.BlockSpec((1, tm, SSM_WIDTH), lambda b, i: (b, i, 0)),
                   pl.BlockSpec((1, tm, 3 * ATTN_WIDTH), lambda b, i: (b, i, 0))],
        out_shape=[jax.ShapeDtypeStruct((nb, seq, SSM_WIDTH), F32),
                   jax.ShapeDtypeStruct((nb, seq, 3 * ATTN_WIDTH), BF16)],
        compiler_params=pltpu.CompilerParams(vmem_limit_bytes=VMEM_LIMIT),
        name="in_proj",
    )(x, shift, scale, gain, w_in_bf16)


def _cmul(a, b):
    return a[0] * b[0] - a[1] * b[1], a[0] * b[1] + a[1] * b[0]


def _ssm_matrices(a_re, a_im, log_dt, b_re, b_im, c_re, c_im, d_skip):
    hp = lax.Precision.HIGHEST
    c_len, g_n, p_n, h_n = SSM_CHUNK, N_SSM_GROUPS, SSM_STATE, SSM_GROUP_CH
    lam = (jnp.minimum(a_re.astype(F32), -1e-4), a_im.astype(F32))
    dt = jnp.exp(log_dt.astype(F32))[:, :, None]
    z = (lam[0] * dt, lam[1] * dt)
    j = jnp.arange(c_len + 1, dtype=F32)[None, :, None, None]
    mag = jnp.exp(z[0][:, None] * j)
    apow = (mag * jnp.cos(z[1][:, None] * j), mag * jnp.sin(z[1][:, None] * j))
    lam_sq = lam[0] * lam[0] + lam[1] * lam[1]
    zoh = _cmul((apow[0][:, 1] - 1.0, apow[1][:, 1]), (lam[0] / lam_sq, -lam[1] / lam_sq))
    b_bar = _cmul((zoh[0][..., None], zoh[1][..., None]), (b_re.astype(F32), b_im.astype(F32)))
    c = (c_re.astype(F32), c_im.astype(F32))

    w = _cmul((apow[0][:, :c_len, :, :, None], apow[1][:, :c_len, :, :, None]),
              (b_bar[0][:, None], b_bar[1][:, None]))
    kern = (jnp.einsum("dghp,djgpi->djghi", c[0], w[0], precision=hp)
            - jnp.einsum("dghp,djgpi->djghi", c[1], w[1], precision=hp))
    s_idx = jnp.arange(c_len)[:, None, None]
    t_idx = jnp.arange(c_len)[None, :, None]
    lag = jnp.arange(c_len)[None, None, :]
    sel_f = (t_idx - s_idx == lag).astype(F32)
    sel_b = (s_idx - t_idx == lag).astype(F32)
    toep = (jnp.einsum("stj,jghi->gsith", sel_f, kern[0], precision=hp)
            + jnp.einsum("stj,jghi->gsith", sel_b, kern[1], precision=hp))
    d_gh = d_skip.astype(F32).reshape(g_n, h_n)
    eye_t = jnp.eye(c_len, dtype=F32)
    eye_h = jnp.eye(h_n, dtype=F32)
    toep = toep + (eye_t[None, :, None, :, None] * eye_h[None, None, :, None, :]
                   * d_gh[:, None, None, None, :])
    toep = toep.reshape(g_n, CHUNK_COLS, CHUNK_COLS)

    pw_f = (apow[0][0, :c_len][::-1][..., None], apow[1][0, :c_len][::-1][..., None])
    pw_b = (apow[0][1, :c_len][..., None], apow[1][1, :c_len][..., None])
    e_f = _cmul(pw_f, (b_bar[0][0][None], b_bar[1][0][None]))
    e_b = _cmul(pw_b, (b_bar[0][1][None], b_bar[1][1][None]))
    bm = jnp.stack([e_f[0], e_b[0], e_f[1], e_b[1]], axis=0)
    bm = jnp.transpose(bm, (2, 1, 4, 0, 3)).reshape(g_n, CHUNK_COLS, 4 * p_n)

    q_f = (apow[0][0, 1:][:, :, None, :], apow[1][0, 1:][:, :, None, :])
    q_b = (apow[0][1, 1:][::-1][:, :, None, :], apow[1][1, 1:][::-1][:, :, None, :])
    w_f = _cmul((c[0][0][None], c[1][0][None]), q_f)
    w_b = _cmul((c[0][1][None], c[1][1][None]), q_b)
    cm = jnp.stack([w_f[0], w_b[0], -w_f[1], -w_b[1]], axis=0)
    cm = jnp.transpose(cm, (2, 0, 4, 1, 3)).reshape(g_n, 4 * p_n, CHUNK_COLS)

    a16_re = jnp.concatenate([apow[0][0, c_len], apow[0][1, c_len]], axis=-1)[:, None, :]
    a16_im = jnp.concatenate([apow[1][0, c_len], apow[1][1, c_len]], axis=-1)[:, None, :]
    tcm = jnp.concatenate([toep, cm], axis=1)
    return bm.astype(BF16), tcm.astype(BF16), a16_re, a16_im


def _ssm_kernel(u_ref, bm_ref, tcm_ref, are_ref, aim_ref, y_ref, s_ref, x_ref, *, nb, n_chunks):
    ub = u_ref[0].astype(BF16)
    s_ref[...] = jnp.dot(ub, bm_ref[0], preferred_element_type=F32)
    a_re = jnp.broadcast_to(are_ref[0], (nb, LANES))
    a_im = jnp.broadcast_to(aim_ref[0], (nb, LANES))
    is_fwd = lax.broadcasted_iota(jnp.int32, (nb, LANES), 1) < SSM_STATE
    half = SSM_STATE

    def step(i, carry):
        x_re, x_im = carry
        rf = pl.ds(pl.multiple_of(i * nb, nb), nb)
        rb = pl.ds(pl.multiple_of((n_chunks - 1 - i) * nb, nb), nb)
        x_ref[rf, 0:half] = x_re[:, 0:half]
        x_ref[rf, LANES:LANES + half] = x_im[:, 0:half]
        x_ref[rb, half:LANES] = x_re[:, half:LANES]
        x_ref[rb, LANES + half:2 * LANES] = x_im[:, half:LANES]
        s_re = jnp.where(is_fwd, s_ref[rf, 0:LANES], s_ref[rb, 0:LANES])
        s_im = jnp.where(is_fwd, s_ref[rf, LANES:2 * LANES], s_ref[rb, LANES:2 * LANES])
        n_re = a_re * x_re - a_im * x_im + s_re
        n_im = a_re * x_im + a_im * x_re + s_im
        return n_re, n_im

    zero = jnp.zeros((nb, LANES), F32)
    lax.fori_loop(0, n_chunks, step, (zero, zero))
    y = jnp.dot(ub, tcm_ref[0, :CHUNK_COLS, :], preferred_element_type=F32)
    y = y + jnp.dot(x_ref[...].astype(BF16), tcm_ref[0, CHUNK_COLS:, :], preferred_element_type=F32)
    y_ref[0] = y


def _ssm(u_g, bm, tcm, a16_re, a16_im, nb):
    g_n, rows, _ = u_g.shape
    n_chunks = rows // nb
    per_g = lambda g: (g, 0, 0)
    return pl.pallas_call(
        functools.partial(_ssm_kernel, nb=nb, n_chunks=n_chunks),
        grid=(g_n,),
        in_specs=[pl.BlockSpec((1, rows, CHUNK_COLS), per_g),
                  pl.BlockSpec((1, CHUNK_COLS, 4 * SSM_STATE), per_g),
                  pl.BlockSpec((1, CHUNK_COLS + 4 * SSM_STATE, CHUNK_COLS), per_g),
                  pl.BlockSpec((1, 1, LANES), per_g),
                  pl.BlockSpec((1, 1, LANES), per_g)],
        out_specs=pl.BlockSpec((1, rows, CHUNK_COLS), per_g),
        out_shape=jax.ShapeDtypeStruct((g_n, rows, CHUNK_COLS), F32),
        scratch_shapes=[pltpu.VMEM((rows, 4 * SSM_STATE), F32),
                        pltpu.VMEM((rows, 4 * SSM_STATE), F32)],
        compiler_params=pltpu.CompilerParams(vmem_limit_bytes=VMEM_LIMIT),
        name="ssm_chunked",
    )(u_g, bm, tcm, a16_re, a16_im)


def _attn_bias_table(rpb):
    wh = NA_WIN_ROWS
    cols = jnp.arange(GRID_W)
    col_start = jnp.clip(cols - NA_WIN_COLS // 2, 0, GRID_W - NA_WIN_COLS)
    kc = jnp.arange(GRID_W)
    valid = (kc[None, :] >= col_start[:, None]) & (kc[None, :] < col_start[:, None] + NA_WIN_COLS)
    dc = kc[None, :] - cols[:, None] + NA_WIN_COLS - 1
    case = jnp.arange(wh)
    i = jnp.arange(wh)
    dr = i[None, :] - case[:, None] + NA_WIN_ROWS - 1
    sel_r = (dr[:, :, None] == jnp.arange(2 * NA_WIN_ROWS - 1)).astype(F32)
    sel_c = ((dc[:, :, None] == jnp.arange(2 * NA_WIN_COLS - 1)) & valid[:, :, None]).astype(F32)
    tab = jnp.einsum("hrd,air,ckd->hacik", rpb.astype(F32), sel_r, sel_c,
                     precision=lax.Precision.HIGHEST)
    tab = jnp.where(valid[None, None, :, None, :], tab, MASK_VALUE)
    tab = tab.reshape(N_HEAD_BLOCKS, HEADS_PER_BLOCK, wh, GRID_W, wh * GRID_W)
    return jnp.transpose(tab, (0, 2, 1, 3, 4)).reshape(N_HEAD_BLOCKS, wh, HEADS_PER_BLOCK * GRID_W, wh * GRID_W)


def _attn_kernel(q_ref, k_ref, v_ref, bias_ref, o_ref, *, rows_per_step, n_rows):
    wh = min(NA_WIN_ROWS, n_rows)
    lane = lax.broadcasted_iota(jnp.int32, (GRID_W, LANES), 1)
    first_head = lane < HEAD_DIM
    zero = jnp.zeros((GRID_W, LANES), BF16)
    kv_rows, scores = [], []
    for j in range(rows_per_step):
        r = pl.program_id(2) * rows_per_step + j
        rs = jnp.clip(r - wh // 2, 0, n_rows - wh)
        q = q_ref[0, pl.ds(pl.multiple_of(r * GRID_W, GRID_W), GRID_W), :]
        qs = jnp.concatenate([jnp.where(first_head, q, zero), jnp.where(first_head, zero, q)], axis=0)
        kv_rows.append(pl.ds(pl.multiple_of(rs * GRID_W, GRID_W), wh * GRID_W))
        kb = k_ref[0, kv_rows[j], :]
        s = lax.dot_general(qs, kb, (((1,), (1,)), ((), ())), preferred_element_type=F32)
        scores.append(s + bias_ref[0, r - rs])
    probs, denoms = [], []
    for s in scores:
        p = jnp.exp(s - jnp.max(s, axis=-1, keepdims=True))
        denoms.append(jnp.sum(p, axis=-1, keepdims=True))
        probs.append(p.astype(BF16))
    for j in range(rows_per_step):
        o = jnp.dot(probs[j], v_ref[0, kv_rows[j], :], preferred_element_type=F32) / denoms[j]
        o_ref[0, j * GRID_W:(j + 1) * GRID_W, :] = jnp.where(first_head, o[:GRID_W], o[GRID_W:])


def _attention(qkv, bias_tab, rows_per_step=8):
    nb, seq, _ = qkv.shape
    n_rows = seq // GRID_W
    wh = min(NA_WIN_ROWS, n_rows)
    assert wh == NA_WIN_ROWS and n_rows % rows_per_step == 0
    nhb = N_HEAD_BLOCKS
    return pl.pallas_call(
        functools.partial(_attn_kernel, rows_per_step=rows_per_step, n_rows=n_rows),
        grid=(nhb, nb, n_rows // rows_per_step),
        in_specs=[pl.BlockSpec((1, seq, LANES), lambda h, b, r: (b, 0, h)),
                  pl.BlockSpec((1, seq, LANES), lambda h, b, r: (b, 0, nhb + h)),
                  pl.BlockSpec((1, seq, LANES), lambda h, b, r: (b, 0, 2 * nhb + h)),
                  pl.BlockSpec((1, wh, HEADS_PER_BLOCK * GRID_W, wh * GRID_W), lambda h, b, r: (h, 0, 0, 0))],
        out_specs=pl.BlockSpec((1, rows_per_step * GRID_W, LANES), lambda h, b, r: (b, r, h)),
        out_shape=jax.ShapeDtypeStruct((nb, seq, ATTN_WIDTH), F32),
        compiler_params=pltpu.CompilerParams(vmem_limit_bytes=VMEM_LIMIT),
        name="nbr_attention",
    )(qkv, qkv, qkv, bias_tab)


def _gelu_tanh(x):
    return 0.5 * x * (1.0 + jnp.tanh(math.sqrt(2.0 / math.pi) * (x + 0.044715 * (x * x * x))))


def _mix_kernel(x_ref, ys_ref, ya_ref, g_ref, wglu_ref, bglu_ref, ns_ref, na_ref, wout_ref, o_ref):
    ys = _gelu_tanh(ys_ref[0])
    z = jnp.dot(ys.astype(BF16), wglu_ref[...], preferred_element_type=F32) + bglu_ref[...]
    ys = ys * jax.nn.sigmoid(z)
    m_s = _rms(ys, ns_ref[...]).astype(BF16)
    m_a = _rms(ya_ref[0], na_ref[...]).astype(BF16)
    o = jnp.dot(m_s, wout_ref[:SSM_WIDTH, :], preferred_element_type=F32)
    o = o + jnp.dot(m_a, wout_ref[SSM_WIDTH:, :], preferred_element_type=F32)
    o_ref[0] = x_ref[0] + g_ref[0] * o


def _mix_out(x, y_ssm, y_att, g_mix, w_glu, b_glu, n_ssm, n_att, w_out, tm=512):
    nb, seq, _ = x.shape
    tok = lambda w: pl.BlockSpec((1, tm, w), lambda b, i: (b, i, 0))
    const = lambda shape: pl.BlockSpec(shape, lambda b, i: (0,) * len(shape))
    return pl.pallas_call(
        _mix_kernel,
        grid=(nb, seq // tm),
        in_specs=[tok(D_MODEL), tok(SSM_WIDTH), tok(ATTN_WIDTH),
                  pl.BlockSpec((1, 1, D_MODEL), lambda b, i: (b, 0, 0)),
                  const((SSM_WIDTH, SSM_WIDTH)), const((1, SSM_WIDTH)),
                  const((1, SSM_WIDTH)), const((1, ATTN_WIDTH)),
                  const((D_MODEL, D_MODEL))],
        out_specs=tok(D_MODEL),
        out_shape=jax.ShapeDtypeStruct((nb, seq, D_MODEL), F32),
        compiler_params=pltpu.CompilerParams(vmem_limit_bytes=VMEM_LIMIT),
        name="mix_out",
    )(x, y_ssm, y_att, g_mix, w_glu, b_glu, n_ssm, n_att, w_out)


def _ffn_kernel(x_ref, sh_ref, sc_ref, g_ref, nf_ref, wg_ref, wu_ref, wd_ref, nfin_ref, o_ref):
    x = x_ref[0]
    h = (_rms(x, nf_ref[...]) * (1.0 + sc_ref[0]) + sh_ref[0]).astype(BF16)
    f = jnp.zeros(x.shape, F32)
    for j in range(D_FF // FF_CHUNK):
        cols = slice(j * FF_CHUNK, (j + 1) * FF_CHUNK)
        gate = jnp.dot(h, wg_ref[:, cols], preferred_element_type=F32)
        up = jnp.dot(h, wu_ref[:, cols], preferred_element_type=F32)
        act = (gate * jax.nn.sigmoid(gate) * up).astype(BF16)
        f = f + jnp.dot(act, wd_ref[cols, :], preferred_element_type=F32)
    o_ref[0] = _rms(x + g_ref[0] * f, nfin_ref[...])


def _ffn(x, shift, scale, gate, n_ffn, w_gate, w_up, w_down, n_final, tm=512):
    nb, seq, _ = x.shape
    tok = pl.BlockSpec((1, tm, D_MODEL), lambda b, i: (b, i, 0))
    vec = pl.BlockSpec((1, 1, D_MODEL), lambda b, i: (b, 0, 0))
    const = lambda shape: pl.BlockSpec(shape, lambda b, i: (0,) * len(shape),
                                       pipeline_mode=pl.Buffered(1))
    return pl.pallas_call(
        _ffn_kernel,
        grid=(nb, seq // tm),
        in_specs=[tok, vec, vec, vec, const((1, D_MODEL)),
                  const((D_MODEL, D_FF)), const((D_MODEL, D_FF)), const((D_FF, D_MODEL)),
                  const((1, D_MODEL))],
        out_specs=tok,
        out_shape=jax.ShapeDtypeStruct((nb, seq, D_MODEL), F32),
        compiler_params=pltpu.CompilerParams(vmem_limit_bytes=VMEM_LIMIT),
        name="ffn_final",
    )(x, shift, scale, gate, n_ffn, w_gate, w_up, w_down, n_final)


def _to_group_layout(u):
    nb, seq, _ = u.shape
    n_chunks = seq // SSM_CHUNK
    u5 = u.reshape(nb, n_chunks, SSM_CHUNK, N_SSM_GROUPS, SSM_GROUP_CH)
    return jnp.transpose(u5, (3, 1, 0, 2, 4)).reshape(N_SSM_GROUPS, n_chunks * nb, CHUNK_COLS)


def _from_group_layout(y_g, nb):
    n_chunks = y_g.shape[1] // nb
    y5 = y_g.reshape(N_SSM_GROUPS, n_chunks, nb, SSM_CHUNK, SSM_GROUP_CH)
    return jnp.transpose(y5, (2, 1, 3, 0, 4)).reshape(nb, n_chunks * SSM_CHUNK, SSM_WIDTH)


def _trunk(x, mod, p):
    nb = x.shape[0]
    sh_mix, sc_mix, g_mix, sh_ffn, sc_ffn, g_ffn = [
        mod[:, i * D_MODEL:(i + 1) * D_MODEL].reshape(nb, 1, D_MODEL) for i in range(N_MOD)]
    u, qkv = _in_proj(x, sh_mix, sc_mix, p["norm_mix"], p["w_in"])
    y_g = _ssm(_to_group_layout(u), p["bm"], p["tcm"], p["a16_re"], p["a16_im"], nb)
    y_ssm = _from_group_layout(y_g, nb)
    y_att = _attention(qkv, p["bias_tab"])
    x1 = _mix_out(x, y_ssm, y_att, g_mix, p["w_glu"], p["b_glu"], p["norm_ssm_out"],
                  p["norm_attn_out"], p["w_out"])
    return _ffn(x1, sh_ffn, sc_ffn, g_ffn, p["norm_ffn"], p["w_ffn_gate"], p["w_ffn_up"],
                p["w_ffn_down"], p["norm_final"])


def kernel(x_prompt, x_sample, c_prompt, c_sample, w_ada, b_ada, norm_mix, w_in, ssm_a_re, ssm_a_im, ssm_log_dt, ssm_b_re, ssm_b_im, ssm_c_re, ssm_c_im, ssm_d, w_glu, b_glu, norm_ssm_out, na_rpb, norm_attn_out, w_out, norm_ffn, w_ffn_gate, w_ffn_up, w_ffn_down, norm_final):
    assert w_ada.shape[0] == 1, "single-layer trunk"
    row = lambda v: v.reshape(1, -1).astype(F32)
    bm, tcm, a16_re, a16_im = _ssm_matrices(ssm_a_re[0], ssm_a_im[0], ssm_log_dt[0], ssm_b_re[0],
                                            ssm_b_im[0], ssm_c_re[0], ssm_c_im[0], ssm_d[0])
    p = dict(
        norm_mix=row(norm_mix[0]), w_in=w_in[0].astype(BF16),
        bm=bm, tcm=tcm, a16_re=a16_re, a16_im=a16_im,
        bias_tab=_attn_bias_table(na_rpb[0]),
        w_glu=w_glu[0].astype(BF16), b_glu=row(b_glu[0]),
        norm_ssm_out=row(norm_ssm_out[0]), norm_attn_out=row(norm_attn_out[0]),
        w_out=w_out[0].astype(BF16), norm_ffn=row(norm_ffn[0]),
        w_ffn_gate=w_ffn_gate[0].astype(BF16), w_ffn_up=w_ffn_up[0].astype(BF16),
        w_ffn_down=w_ffn_down[0].astype(BF16), norm_final=row(norm_final),
    )
    n_prompt = x_prompt.shape[0]
    mod = _ada_mod(jnp.concatenate([c_prompt, c_sample], axis=0), w_ada[0], b_ada[0])
    y_prompt = _trunk(x_prompt, mod[:n_prompt], p)
    y_sample = _trunk(x_sample, mod[n_prompt:], p)
    return (y_prompt, y_sample)
```

```python
import functools
import math

import jax
import jax.numpy as jnp
from jax import lax
from jax.experimental import pallas as pl
from jax.experimental.pallas import tpu as pltpu

D_MODEL = 1024
GRID_W = 64
SSM_WIDTH = D_MODEL // 2
SSM_GROUP_CH = 16
N_SSM_GROUPS = SSM_WIDTH // SSM_GROUP_CH
SSM_STATE = 64
ATTN_WIDTH = D_MODEL - SSM_WIDTH
HEAD_DIM = 64
N_HEADS_ATTN = ATTN_WIDTH // HEAD_DIM
IN_PROJ_WIDTH = SSM_WIDTH + 3 * ATTN_WIDTH
NA_WIN_ROWS = 8
NA_WIN_COLS = 16
D_FF = -(-8 * D_MODEL // (3 * 256)) * 256
N_MOD = 6
EPS = 1e-6

LANES = 128
SSM_CHUNK = 16
CHUNK_COLS = SSM_CHUNK * SSM_GROUP_CH
PIECES = LANES // SSM_GROUP_CH
TOK_PITCH = 24
BATCH_PITCH_PAD = 8
HEADS_PER_BLOCK = LANES // HEAD_DIM
N_HEAD_BLOCKS = N_HEADS_ATTN // HEADS_PER_BLOCK
FF_CHUNK = 256
MASK_VALUE = -1e30
VMEM_LIMIT = 56 * 1024 * 1024

F32 = jnp.float32
BF16 = jnp.bfloat16


def _rms(x, gain):
    return x * lax.rsqrt(jnp.mean(x * x, axis=-1, keepdims=True) + EPS) * gain


def _ada_kernel(c_ref, w_ref, b_ref, o_ref):
    c = c_ref[...]
    s = c * jax.nn.sigmoid(c)
    o_ref[...] = jnp.dot(s, w_ref[...], preferred_element_type=F32,
                         precision=lax.Precision.HIGHEST) + b_ref[...]


def _ada_mod(c, w_ada, b_ada):
    nb = c.shape[0]
    n_out = w_ada.shape[1]
    tn = 1536
    return pl.pallas_call(
        _ada_kernel,
        grid=(n_out // tn,),
        in_specs=[pl.BlockSpec((nb, D_MODEL), lambda j: (0, 0)),
                  pl.BlockSpec((D_MODEL, tn), lambda j: (0, j)),
                  pl.BlockSpec((1, tn), lambda j: (0, j))],
        out_specs=pl.BlockSpec((nb, tn), lambda j: (0, j)),
        out_shape=jax.ShapeDtypeStruct((nb, n_out), F32),
        name="ada_mod",
    )(c, w_ada, b_ada.reshape(1, n_out))


def _piece_transpose(xs):
    assert len(xs) == PIECES
    piece = lax.broadcasted_iota(jnp.int32, xs[0].shape, 1) // SSM_GROUP_CH
    xs = list(xs)
    for d in (4, 2, 1):
        keep = (piece & d) == 0
        nxt = list(xs)
        for a in range(len(xs)):
            if a & d:
                continue
            b = a + d
            nxt[a] = jnp.where(keep, xs[a], pltpu.roll(xs[b], d * SSM_GROUP_CH, axis=1))
            nxt[b] = jnp.where(keep, pltpu.roll(xs[a], LANES - d * SSM_GROUP_CH, axis=1), xs[b])
        xs = nxt
    return xs


def _inproj_kernel(x_ref, sh_ref, sc_ref, g_ref, w_ref, ug_ref, qkv_ref, u_scr):
    x = x_ref[0]
    h = _rms(x, g_ref[...]) * (1.0 + sc_ref[0]) + sh_ref[0]
    hb = h.astype(BF16)
    u = jnp.dot(hb, w_ref[:, :SSM_WIDTH], preferred_element_type=F32)
    n_chunks = u.shape[0] // SSM_CHUNK
    half = PIECES
    for lb in range(SSM_WIDTH // LANES):
        for c in range(n_chunks):
            u_scr[lb, c * TOK_PITCH:c * TOK_PITCH + SSM_CHUNK, :] = (
                u[c * SSM_CHUNK:(c + 1) * SSM_CHUNK, lb * LANES:(lb + 1) * LANES])
    for lb in range(SSM_WIDTH // LANES):
        for th in range(2):
            xs = [u_scr[lb, pl.ds(th * half + i, n_chunks, stride=TOK_PITCH), :] for i in range(half)]
            ys = _piece_transpose(xs)
            for gl in range(half):
                ug_ref[lb * half + gl, :, th * LANES:(th + 1) * LANES] = ys[gl]
    for j in range(3):
        lo = SSM_WIDTH + j * ATTN_WIDTH
        r = jnp.dot(hb, w_ref[:, lo:lo + ATTN_WIDTH], preferred_element_type=F32)
        if j == 0:
            r = r * (HEAD_DIM ** -0.5)
        qkv_ref[0, :, j * ATTN_WIDTH:(j + 1) * ATTN_WIDTH] = r.astype(BF16)


def _in_proj(x, shift, scale, gain, w_in_bf16, tm=512):
    nb, seq, _ = x.shape
    vec = pl.BlockSpec((1, 1, D_MODEL), lambda b, i: (b, 0, 0))
    steps = seq // tm
    blk_chunks = tm // SSM_CHUNK
    return pl.pallas_call(
        _inproj_kernel,
        grid=(nb, steps),
        in_specs=[pl.BlockSpec((1, tm, D_MODEL), lambda b, i: (b, i, 0)),
                  vec, vec,
                  pl.BlockSpec((1, D_MODEL), lambda b, i: (0, 0)),
                  pl.BlockSpec((D_MODEL, IN_PROJ_WIDTH), lambda b, i: (0, 0))],
        out_specs=[pl.BlockSpec((N_SSM_GROUPS, blk_chunks, CHUNK_COLS), lambda b, i: (0, b * steps + i, 0)),
                   pl.BlockSpec((1, tm, 3 * ATTN_WIDTH), lambda b, i: (b, i, 0))],
        out_shape=[jax.ShapeDtypeStruct((N_SSM_GROUPS, nb * seq // SSM_CHUNK, CHUNK_COLS), F32),
                   jax.ShapeDtypeStruct((nb, seq, 3 * ATTN_WIDTH), BF16)],
        scratch_shapes=[pltpu.VMEM((SSM_WIDTH // LANES, blk_chunks * TOK_PITCH, LANES), F32)],
        compiler_params=pltpu.CompilerParams(vmem_limit_bytes=VMEM_LIMIT),
        name="in_proj",
    )(x, shift, scale, gain, w_in_bf16)


def _cmul(a, b):
    return a[0] * b[0] - a[1] * b[1], a[0] * b[1] + a[1] * b[0]


def _ssm_matrices(a_re, a_im, log_dt, b_re, b_im, c_re, c_im, d_skip):
    hp = lax.Precision.HIGHEST
    c_len, g_n, p_n, h_n = SSM_CHUNK, N_SSM_GROUPS, SSM_STATE, SSM_GROUP_CH
    lam = (jnp.minimum(a_re.astype(F32), -1e-4), a_im.astype(F32))
    dt = jnp.exp(log_dt.astype(F32))[:, :, None]
    z = (lam[0] * dt, lam[1] * dt)
    j = jnp.arange(c_len + 1, dtype=F32)[None, :, None, None]
    mag = jnp.exp(z[0][:, None] * j)
    apow = (mag * jnp.cos(z[1][:, None] * j), mag * jnp.sin(z[1][:, None] * j))
    lam_sq = lam[0] * lam[0] + lam[1] * lam[1]
    zoh = _cmul((apow[0][:, 1] - 1.0, apow[1][:, 1]), (lam[0] / lam_sq, -lam[1] / lam_sq))
    b_bar = _cmul((zoh[0][..., None], zoh[1][..., None]), (b_re.astype(F32), b_im.astype(F32)))
    c = (c_re.astype(F32), c_im.astype(F32))

    w = _cmul((apow[0][:, :c_len, :, :, None], apow[1][:, :c_len, :, :, None]),
              (b_bar[0][:, None], b_bar[1][:, None]))
    kern = (jnp.einsum("dghp,djgpi->djghi", c[0], w[0], precision=hp)
            - jnp.einsum("dghp,djgpi->djghi", c[1], w[1], precision=hp))
    s_idx = jnp.arange(c_len)[:, None, None]
    t_idx = jnp.arange(c_len)[None, :, None]
    lag = jnp.arange(c_len)[None, None, :]
    sel_f = (t_idx - s_idx == lag).astype(F32)
    sel_b = (s_idx - t_idx == lag).astype(F32)
    toep = (jnp.einsum("stj,jghi->gsith", sel_f, kern[0], precision=hp)
            + jnp.einsum("stj,jghi->gsith", sel_b, kern[1], precision=hp))
    d_gh = d_skip.astype(F32).reshape(g_n, h_n)
    eye_t = jnp.eye(c_len, dtype=F32)
    eye_h = jnp.eye(h_n, dtype=F32)
    toep = toep + (eye_t[None, :, None, :, None] * eye_h[None, None, :, None, :]
                   * d_gh[:, None, None, None, :])
    toep = toep.reshape(g_n, CHUNK_COLS, CHUNK_COLS)

    pw_f = (apow[0][0, :c_len][::-1][..., None], apow[1][0, :c_len][::-1][..., None])
    pw_b = (apow[0][1, :c_len][..., None], apow[1][1, :c_len][..., None])
    e_f = _cmul(pw_f, (b_bar[0][0][None], b_bar[1][0][None]))
    e_b = _cmul(pw_b, (b_bar[0][1][None], b_bar[1][1][None]))
    bm = jnp.stack([e_f[0], e_b[0], e_f[1], e_b[1]], axis=0)
    bm = jnp.transpose(bm, (2, 1, 4, 0, 3)).reshape(g_n, CHUNK_COLS, 4 * p_n)

    q_f = (apow[0][0, 1:][:, :, None, :], apow[1][0, 1:][:, :, None, :])
    q_b = (apow[0][1, 1:][::-1][:, :, None, :], apow[1][1, 1:][::-1][:, :, None, :])
    w_f = _cmul((c[0][0][None], c[1][0][None]), q_f)
    w_b = _cmul((c[0][1][None], c[1][1][None]), q_b)
    cm = jnp.stack([w_f[0], w_b[0], -w_f[1], -w_b[1]], axis=0)
    cm = jnp.transpose(cm, (2, 0, 4, 1, 3)).reshape(g_n, 4 * p_n, CHUNK_COLS)

    a16_re = jnp.concatenate([apow[0][0, c_len], apow[0][1, c_len]], axis=-1)[:, None, :]
    a16_im = jnp.concatenate([apow[1][0, c_len], apow[1][1, c_len]], axis=-1)[:, None, :]
    tcm = jnp.concatenate([toep, cm], axis=1)
    return bm.astype(BF16), tcm.astype(BF16), a16_re, a16_im


def _ssm_kernel(u_ref, bm_ref, tcm_ref, are_ref, aim_ref, y_ref, s_scr, xf_scr, xb_scr, *, nb, n_chunks):
    pitch = n_chunks + BATCH_PITCH_PAD
    ub = u_ref[0].astype(BF16)
    s = jnp.dot(ub, bm_ref[0], preferred_element_type=F32)
    for b in range(nb):
        for part in range(2):
            s_scr[part, b * pitch:b * pitch + n_chunks, :] = (
                s[b * n_chunks:(b + 1) * n_chunks, part * LANES:(part + 1) * LANES])
    a_re = jnp.broadcast_to(are_ref[0], (nb, LANES))
    a_im = jnp.broadcast_to(aim_ref[0], (nb, LANES))
    is_fwd = lax.broadcasted_iota(jnp.int32, (nb, LANES), 1) < SSM_STATE

    def step(i, carry):
        x_re, x_im = carry
        rf = pl.ds(i, nb, stride=pitch)
        rb = pl.ds(n_chunks - 1 - i, nb, stride=pitch)
        xf_scr[0, rf, :] = x_re
        xf_scr[1, rf, :] = x_im
        xb_scr[0, rb, :] = x_re
        xb_scr[1, rb, :] = x_im
        s_re = jnp.where(is_fwd, s_scr[0, rf, :], s_scr[0, rb, :])
        s_im = jnp.where(is_fwd, s_scr[1, rf, :], s_scr[1, rb, :])
        n_re = a_re * x_re - a_im * x_im + s_re
        n_im = a_re * x_im + a_im * x_re + s_im
        return n_re, n_im

    zero = jnp.zeros((nb, LANES), F32)
    lax.fori_loop(0, n_chunks, step, (zero, zero))
    fwd_rows = lax.broadcasted_iota(jnp.int32, (n_chunks, LANES), 1) < SSM_STATE
    x_in = []
    for b in range(nb):
        rows = slice(b * pitch, b * pitch + n_chunks)
        x_in.append(jnp.concatenate(
            [jnp.where(fwd_rows, xf_scr[part, rows, :], xb_scr[part, rows, :]) for part in range(2)], axis=1))
    x_in = jnp.concatenate(x_in, axis=0).astype(BF16)
    y = jnp.dot(ub, tcm_ref[0, :CHUNK_COLS, :], preferred_element_type=F32)
    y = y + jnp.dot(x_in, tcm_ref[0, CHUNK_COLS:, :], preferred_element_type=F32)
    y_ref[0] = y


def _ssm(u_g, bm, tcm, a16_re, a16_im, nb):
    g_n, rows, _ = u_g.shape
    n_chunks = rows // nb
    per_g = lambda g: (g, 0, 0)
    state_scr = pltpu.VMEM((2, nb * (n_chunks + BATCH_PITCH_PAD), LANES), F32)
    return pl.pallas_call(
        functools.partial(_ssm_kernel, nb=nb, n_chunks=n_chunks),
        grid=(g_n,),
        in_specs=[pl.BlockSpec((1, rows, CHUNK_COLS), per_g),
                  pl.BlockSpec((1, CHUNK_COLS, 4 * SSM_STATE), per_g),
                  pl.BlockSpec((1, CHUNK_COLS + 4 * SSM_STATE, CHUNK_COLS), per_g),
                  pl.BlockSpec((1, 1, LANES), per_g),
                  pl.BlockSpec((1, 1, LANES), per_g)],
        out_specs=pl.BlockSpec((1, rows, CHUNK_COLS), per_g),
        out_shape=jax.ShapeDtypeStruct((g_n, rows, CHUNK_COLS), F32),
        scratch_shapes=[state_scr, state_scr, state_scr],
        compiler_params=pltpu.CompilerParams(vmem_limit_bytes=VMEM_LIMIT),
        name="ssm_chunked",
    )(u_g, bm, tcm, a16_re, a16_im)


def _attn_bias_table(rpb):
    wh = NA_WIN_ROWS
    cols = jnp.arange(GRID_W)
    col_start = jnp.clip(cols - NA_WIN_COLS // 2, 0, GRID_W - NA_WIN_COLS)
    kc = jnp.arange(GRID_W)
    valid = (kc[None, :] >= col_start[:, None]) & (kc[None, :] < col_start[:, None] + NA_WIN_COLS)
    dc = kc[None, :] - cols[:, None] + NA_WIN_COLS - 1
    case = jnp.arange(wh)
    i = jnp.arange(wh)
    dr = i[None, :] - case[:, None] + NA_WIN_ROWS - 1
    sel_r = (dr[:, :, None] == jnp.arange(2 * NA_WIN_ROWS - 1)).astype(F32)
    sel_c = ((dc[:, :, None] == jnp.arange(2 * NA_WIN_COLS - 1)) & valid[:, :, None]).astype(F32)
    tab = jnp.einsum("hrd,air,ckd->hacik", rpb.astype(F32), sel_r, sel_c,
                     precision=lax.Precision.HIGHEST)
    tab = jnp.where(valid[None, None, :, None, :], tab, MASK_VALUE)
    tab = tab.reshape(N_HEAD_BLOCKS, HEADS_PER_BLOCK, wh, GRID_W, wh * GRID_W)
    return jnp.transpose(tab, (0, 2, 1, 3, 4)).reshape(N_HEAD_BLOCKS, wh, HEADS_PER_BLOCK * GRID_W, wh * GRID_W)


def _attn_kernel(q_ref, k_ref, v_ref, bias_ref, o_ref, *, rows_per_step, n_rows):
    wh = min(NA_WIN_ROWS, n_rows)
    lane = lax.broadcasted_iota(jnp.int32, (GRID_W, LANES), 1)
    first_head = lane < HEAD_DIM
    zero = jnp.zeros((GRID_W, LANES), BF16)
    kv_rows, scores = [], []
    for j in range(rows_per_step):
        r = pl.program_id(2) * rows_per_step + j
        rs = jnp.clip(r - wh // 2, 0, n_rows - wh)
        q = q_ref[0, pl.ds(pl.multiple_of(r * GRID_W, GRID_W), GRID_W), :]
        qs = jnp.concatenate([jnp.where(first_head, q, zero), jnp.where(first_head, zero, q)], axis=0)
        kv_rows.append(pl.ds(pl.multiple_of(rs * GRID_W, GRID_W), wh * GRID_W))
        kb = k_ref[0, kv_rows[j], :]
        s = lax.dot_general(qs, kb, (((1,), (1,)), ((), ())), preferred_element_type=F32)
        scores.append(s + bias_ref[0, r - rs])
    probs, denoms = [], []
    for s in scores:
        p = jnp.exp(s - jnp.max(s, axis=-1, keepdims=True))
        denoms.append(jnp.sum(p, axis=-1, keepdims=True))
        probs.append(p.astype(BF16))
    for j in range(rows_per_step):
        o = jnp.dot(probs[j], v_ref[0, kv_rows[j], :], preferred_element_type=F32) / denoms[j]
        o_ref[0, j * GRID_W:(j + 1) * GRID_W, :] = jnp.where(first_head, o[:GRID_W], o[GRID_W:])


def _attention(qkv, bias_tab, rows_per_step=8):
    nb, seq, _ = qkv.shape
    n_rows = seq // GRID_W
    wh = min(NA_WIN_ROWS, n_rows)
    assert wh == NA_WIN_ROWS and n_rows % rows_per_step == 0
    nhb = N_HEAD_BLOCKS
    return pl.pallas_call(
        functools.partial(_attn_kernel, rows_per_step=rows_per_step, n_rows=n_rows),
        grid=(nhb, nb, n_rows // rows_per_step),
        in_specs=[pl.BlockSpec((1, seq, LANES), lambda h, b, r: (b, 0, h)),
                  pl.BlockSpec((1, seq, LANES), lambda h, b, r: (b, 0, nhb + h)),
                  pl.BlockSpec((1, seq, LANES), lambda h, b, r: (b, 0, 2 * nhb + h)),
                  pl.BlockSpec((1, wh, HEADS_PER_BLOCK * GRID_W, wh * GRID_W), lambda h, b, r: (h, 0, 0, 0))],
        out_specs=pl.BlockSpec((1, rows_per_step * GRID_W, LANES), lambda h, b, r: (b, r, h)),
        out_shape=jax.ShapeDtypeStruct((nb, seq, ATTN_WIDTH), F32),
        compiler_params=pltpu.CompilerParams(vmem_limit_bytes=VMEM_LIMIT),
        name="nbr_attention",
    )(qkv, qkv, qkv, bias_tab)


def _gelu_tanh(x):
    return 0.5 * x * (1.0 + jnp.tanh(math.sqrt(2.0 / math.pi) * (x + 0.044715 * (x * x * x))))


def _mix_kernel(x_ref, yg_ref, ya_ref, g_ref, wglu_ref, bglu_ref, ns_ref, na_ref, wout_ref, o_ref, y_scr):
    n_chunks = yg_ref.shape[1]
    for lb in range(SSM_WIDTH // LANES):
        for th in range(2):
            ys = [yg_ref[lb * PIECES + gl, :, th * LANES:(th + 1) * LANES] for gl in range(PIECES)]
            xs = _piece_transpose(ys)
            for i in range(PIECES):
                y_scr[lb, pl.ds(th * PIECES + i, n_chunks, stride=TOK_PITCH), :] = xs[i]
    y_tok = jnp.concatenate(
        [jnp.concatenate([y_scr[lb, c * TOK_PITCH:c * TOK_PITCH + SSM_CHUNK, :] for c in range(n_chunks)], axis=0)
         for lb in range(SSM_WIDTH // LANES)], axis=1)
    ys = _gelu_tanh(y_tok)
    z = jnp.dot(ys.astype(BF16), wglu_ref[...], preferred_element_type=F32) + bglu_ref[...]
    ys = ys * jax.nn.sigmoid(z)
    m_s = _rms(ys, ns_ref[...]).astype(BF16)
    m_a = _rms(ya_ref[0], na_ref[...]).astype(BF16)
    o = jnp.dot(m_s, wout_ref[:SSM_WIDTH, :], preferred_element_type=F32)
    o = o + jnp.dot(m_a, wout_ref[SSM_WIDTH:, :], preferred_element_type=F32)
    o_ref[0] = x_ref[0] + g_ref[0] * o


def _mix_out(x, y_g, y_att, g_mix, w_glu, b_glu, n_ssm, n_att, w_out, tm=512):
    nb, seq, _ = x.shape
    steps = seq // tm
    blk_chunks = tm // SSM_CHUNK
    tok = lambda w: pl.BlockSpec((1, tm, w), lambda b, i: (b, i, 0))
    const = lambda shape: pl.BlockSpec(shape, lambda b, i: (0,) * len(shape))
    return pl.pallas_call(
        _mix_kernel,
        grid=(nb, steps),
        in_specs=[tok(D_MODEL),
                  pl.BlockSpec((N_SSM_GROUPS, blk_chunks, CHUNK_COLS), lambda b, i: (0, b * steps + i, 0)),
                  tok(ATTN_WIDTH),
                  pl.BlockSpec((1, 1, D_MODEL), lambda b, i: (b, 0, 0)),
                  const((SSM_WIDTH, SSM_WIDTH)), const((1, SSM_WIDTH)),
                  const((1, SSM_WIDTH)), const((1, ATTN_WIDTH)),
                  const((D_MODEL, D_MODEL))],
        out_specs=tok(D_MODEL),
        out_shape=jax.ShapeDtypeStruct((nb, seq, D_MODEL), F32),
        scratch_shapes=[pltpu.VMEM((SSM_WIDTH // LANES, blk_chunks * TOK_PITCH, LANES), F32)],
        compiler_params=pltpu.CompilerParams(vmem_limit_bytes=VMEM_LIMIT),
        name="mix_out",
    )(x, y_g, y_att, g_mix, w_glu, b_glu, n_ssm, n_att, w_out)


def _ffn_kernel(x_ref, sh_ref, sc_ref, g_ref, nf_ref, wg_ref, wu_ref, wd_ref, nfin_ref, o_ref):
    x = x_ref[0]
    h = (_rms(x, nf_ref[...]) * (1.0 + sc_ref[0]) + sh_ref[0]).astype(BF16)
    f = jnp.zeros(x.shape, F32)
    for j in range(D_FF // FF_CHUNK):
        cols = slice(j * FF_CHUNK, (j + 1) * FF_CHUNK)
        gate = jnp.dot(h, wg_ref[:, cols], preferred_element_type=F32)
        up = jnp.dot(h, wu_ref[:, cols], preferred_element_type=F32)
        act = (gate * jax.nn.sigmoid(gate) * up).astype(BF16)
        f = f + jnp.dot(act, wd_ref[cols, :], preferred_element_type=F32)
    o_ref[0] = _rms(x + g_ref[0] * f, nfin_ref[...])


def _ffn(x, shift, scale, gate, n_ffn, w_gate, w_up, w_down, n_final, tm=512):
    nb, seq, _ = x.shape
    tok = pl.BlockSpec((1, tm, D_MODEL), lambda b, i: (b, i, 0))
    vec = pl.BlockSpec((1, 1, D_MODEL), lambda b, i: (b, 0, 0))
    const = lambda shape: pl.BlockSpec(shape, lambda b, i: (0,) * len(shape),
                                       pipeline_mode=pl.Buffered(1))
    return pl.pallas_call(
        _ffn_kernel,
        grid=(nb, seq // tm),
        in_specs=[tok, vec, vec, vec, const((1, D_MODEL)),
                  const((D_MODEL, D_FF)), const((D_MODEL, D_FF)), const((D_FF, D_MODEL)),
                  const((1, D_MODEL))],
        out_specs=tok,
        out_shape=jax.ShapeDtypeStruct((nb, seq, D_MODEL), F32),
        compiler_params=pltpu.CompilerParams(vmem_limit_bytes=VMEM_LIMIT),
        name="ffn_final",
    )(x, shift, scale, gate, n_ffn, w_gate, w_up, w_down, n_final)


def _trunk(x, mod, p):
    nb = x.shape[0]
    sh_mix, sc_mix, g_mix, sh_ffn, sc_ffn, g_ffn = [
        mod[:, i * D_MODEL:(i + 1) * D_MODEL].reshape(nb, 1, D_MODEL) for i in range(N_MOD)]
    u_g, qkv = _in_proj(x, sh_mix, sc_mix, p["norm_mix"], p["w_in"])
    y_g = _ssm(u_g, p["bm"], p["tcm"], p["a16_re"], p["a16_im"], nb)
    y_att = _attention(qkv, p["bias_tab"])
    x1 = _mix_out(x, y_g, y_att, g_mix, p["w_glu"], p["b_glu"], p["norm_ssm_out"],
                  p["norm_attn_out"], p["w_out"])
    return _ffn(x1, sh_ffn, sc_ffn, g_ffn, p["norm_ffn"], p["w_ffn_gate"], p["w_ffn_up"],
                p["w_ffn_down"], p["norm_final"])


def kernel(x_prompt, x_sample, c_prompt, c_sample, w_ada, b_ada, norm_mix, w_in, ssm_a_re, ssm_a_im, ssm_log_dt, ssm_b_re, ssm_b_im, ssm_c_re, ssm_c_im, ssm_d, w_glu, b_glu, norm_ssm_out, na_rpb, norm_attn_out, w_out, norm_ffn, w_ffn_gate, w_ffn_up, w_ffn_down, norm_final):
    assert w_ada.shape[0] == 1, "single-layer trunk"
    row = lambda v: v.reshape(1, -1).astype(F32)
    bm, tcm, a16_re, a16_im = _ssm_matrices(ssm_a_re[0], ssm_a_im[0], ssm_log_dt[0], ssm_b_re[0],
                                            ssm_b_im[0], ssm_c_re[0], ssm_c_im[0], ssm_d[0])
    p = dict(
        norm_mix=row(norm_mix[0]), w_in=w_in[0].astype(BF16),
        bm=bm, tcm=tcm, a16_re=a16_re, a16_im=a16_im,
        bias_tab=_attn_bias_table(na_rpb[0]),
        w_glu=w_glu[0].astype(BF16), b_glu=row(b_glu[0]),
        norm_ssm_out=row(norm_ssm_out[0]), norm_attn_out=row(norm_attn_out[0]),
        w_out=w_out[0].astype(BF16), norm_ffn=row(norm_ffn[0]),
        w_ffn_gate=w_ffn_gate[0].astype(BF16), w_ffn_up=w_ffn_up[0].astype(BF16),
        w_ffn_down=w_ffn_down[0].astype(BF16), norm_final=row(norm_final),
    )
    n_prompt = x_prompt.shape[0]
    mod = _ada_mod(jnp.concatenate([c_prompt, c_sample], axis=0), w_ada[0], b_ada[0])
    y_prompt = _trunk(x_prompt, mod[:n_prompt], p)
    y_sample = _trunk(x_sample, mod[n_prompt:], p)
    return (y_prompt, y_sample)
```

```python
import functools
import math

import jax
import jax.numpy as jnp
from jax import lax
from jax.experimental import pallas as pl
from jax.experimental.pallas import tpu as pltpu

D_MODEL = 1024
GRID_W = 64
SSM_WIDTH = D_MODEL // 2
SSM_GROUP_CH = 16
N_SSM_GROUPS = SSM_WIDTH // SSM_GROUP_CH
SSM_STATE = 64
ATTN_WIDTH = D_MODEL - SSM_WIDTH
HEAD_DIM = 64
N_HEADS_ATTN = ATTN_WIDTH // HEAD_DIM
IN_PROJ_WIDTH = SSM_WIDTH + 3 * ATTN_WIDTH
NA_WIN_ROWS = 8
NA_WIN_COLS = 16
D_FF = -(-8 * D_MODEL // (3 * 256)) * 256
N_MOD = 6
EPS = 1e-6

LANES = 128
SSM_CHUNK = 16
CHUNK_COLS = SSM_CHUNK * SSM_GROUP_CH
PIECES = LANES // SSM_GROUP_CH
TOK_PITCH = 24
BATCH_PITCH_PAD = 8
HEADS_PER_BLOCK = LANES // HEAD_DIM
N_HEAD_BLOCKS = N_HEADS_ATTN // HEADS_PER_BLOCK
FF_CHUNK = 256
MASK_VALUE = -1e30
VMEM_LIMIT = 56 * 1024 * 1024

F32 = jnp.float32
BF16 = jnp.bfloat16


def _rms(x, gain):
    return x * lax.rsqrt(jnp.mean(x * x, axis=-1, keepdims=True) + EPS) * gain


def _ada_kernel(c_ref, w_ref, b_ref, o_ref):
    c = c_ref[...]
    s = c * jax.nn.sigmoid(c)
    o_ref[...] = jnp.dot(s, w_ref[...], preferred_element_type=F32,
                         precision=lax.Precision.HIGHEST) + b_ref[...]


def _ada_mod(c, w_ada, b_ada):
    nb = c.shape[0]
    n_out = w_ada.shape[1]
    tn = 1536
    return pl.pallas_call(
        _ada_kernel,
        grid=(n_out // tn,),
        in_specs=[pl.BlockSpec((nb, D_MODEL), lambda j: (0, 0)),
                  pl.BlockSpec((D_MODEL, tn), lambda j: (0, j)),
                  pl.BlockSpec((1, tn), lambda j: (0, j))],
        out_specs=pl.BlockSpec((nb, tn), lambda j: (0, j)),
        out_shape=jax.ShapeDtypeStruct((nb, n_out), F32),
        name="ada_mod",
    )(c, w_ada, b_ada.reshape(1, n_out))


def _piece_transpose(xs):
    assert len(xs) == PIECES
    piece = lax.broadcasted_iota(jnp.int32, xs[0].shape, 1) // SSM_GROUP_CH
    xs = list(xs)
    for d in (4, 2, 1):
        keep = (piece & d) == 0
        nxt = list(xs)
        for a in range(len(xs)):
            if a & d:
                continue
            b = a + d
            nxt[a] = jnp.where(keep, xs[a], pltpu.roll(xs[b], d * SSM_GROUP_CH, axis=1))
            nxt[b] = jnp.where(keep, pltpu.roll(xs[a], LANES - d * SSM_GROUP_CH, axis=1), xs[b])
        xs = nxt
    return xs


def _inproj_kernel(x_ref, sh_ref, sc_ref, g_ref, w_ref, ug_ref, qkv_ref, u_scr):
    x = x_ref[0]
    h = _rms(x, g_ref[...]) * (1.0 + sc_ref[0]) + sh_ref[0]
    hb = h.astype(BF16)
    u = jnp.dot(hb, w_ref[:, :SSM_WIDTH], preferred_element_type=F32)
    n_chunks = u.shape[0] // SSM_CHUNK
    half = PIECES
    for lb in range(SSM_WIDTH // LANES):
        for c in range(n_chunks):
            u_scr[lb, c * TOK_PITCH:c * TOK_PITCH + SSM_CHUNK, :] = (
                u[c * SSM_CHUNK:(c + 1) * SSM_CHUNK, lb * LANES:(lb + 1) * LANES])
    for lb in range(SSM_WIDTH // LANES):
        for th in range(2):
            xs = [u_scr[lb, pl.ds(th * half + i, n_chunks, stride=TOK_PITCH), :] for i in range(half)]
            ys = _piece_transpose(xs)
            for gl in range(half):
                ug_ref[lb * half + gl, :, th * LANES:(th + 1) * LANES] = ys[gl]
    for j in range(3):
        lo = SSM_WIDTH + j * ATTN_WIDTH
        r = jnp.dot(hb, w_ref[:, lo:lo + ATTN_WIDTH], preferred_element_type=F32)
        if j == 0:
            r = r * (HEAD_DIM ** -0.5)
        qkv_ref[0, :, j * ATTN_WIDTH:(j + 1) * ATTN_WIDTH] = r.astype(BF16)


def _in_proj(x, shift, scale, gain, w_in_bf16, tm=512):
    nb, seq, _ = x.shape
    vec = pl.BlockSpec((1, 1, D_MODEL), lambda b, i: (b, 0, 0))
    steps = seq // tm
    blk_chunks = tm // SSM_CHUNK
    return pl.pallas_call(
        _inproj_kernel,
        grid=(nb, steps),
        in_specs=[pl.BlockSpec((1, tm, D_MODEL), lambda b, i: (b, i, 0)),
                  vec, vec,
                  pl.BlockSpec((1, D_MODEL), lambda b, i: (0, 0)),
                  pl.BlockSpec((D_MODEL, IN_PROJ_WIDTH), lambda b, i: (0, 0))],
        out_specs=[pl.BlockSpec((N_SSM_GROUPS, blk_chunks, CHUNK_COLS), lambda b, i: (0, b * steps + i, 0)),
                   pl.BlockSpec((1, tm, 3 * ATTN_WIDTH), lambda b, i: (b, i, 0))],
        out_shape=[jax.ShapeDtypeStruct((N_SSM_GROUPS, nb * seq // SSM_CHUNK, CHUNK_COLS), F32),
                   jax.ShapeDtypeStruct((nb, seq, 3 * ATTN_WIDTH), BF16)],
        scratch_shapes=[pltpu.VMEM((SSM_WIDTH // LANES, blk_chunks * TOK_PITCH, LANES), F32)],
        compiler_params=pltpu.CompilerParams(vmem_limit_bytes=VMEM_LIMIT),
        name="in_proj",
    )(x, shift, scale, gain, w_in_bf16)


def _cmul(a, b):
    return a[0] * b[0] - a[1] * b[1], a[0] * b[1] + a[1] * b[0]


def _ssm_matrices(a_re, a_im, log_dt, b_re, b_im, c_re, c_im, d_skip):
    hp = lax.Precision.HIGHEST
    c_len, g_n, p_n, h_n = SSM_CHUNK, N_SSM_GROUPS, SSM_STATE, SSM_GROUP_CH
    lam = (jnp.minimum(a_re.astype(F32), -1e-4), a_im.astype(F32))
    dt = jnp.exp(log_dt.astype(F32))[:, :, None]
    z = (lam[0] * dt, lam[1] * dt)
    j = jnp.arange(c_len + 1, dtype=F32)[None, :, None, None]
    mag = jnp.exp(z[0][:, None] * j)
    apow = (mag * jnp.cos(z[1][:, None] * j), mag * jnp.sin(z[1][:, None] * j))
    lam_sq = lam[0] * lam[0] + lam[1] * lam[1]
    zoh = _cmul((apow[0][:, 1] - 1.0, apow[1][:, 1]), (lam[0] / lam_sq, -lam[1] / lam_sq))
    b_bar = _cmul((zoh[0][..., None], zoh[1][..., None]), (b_re.astype(F32), b_im.astype(F32)))
    c_t = (jnp.swapaxes(c_re.astype(F32), 2, 3), jnp.swapaxes(c_im.astype(F32), 2, 3))
    ap_t = (jnp.transpose(apow[0], (0, 2, 3, 1)), jnp.transpose(apow[1], (0, 2, 3, 1)))

    w = _cmul((ap_t[0][..., :c_len, None], ap_t[1][..., :c_len, None]),
              (b_bar[0][:, :, :, None, :], b_bar[1][:, :, :, None, :]))
    kern = jnp.einsum("dgkh,dgkji->dgijh", jnp.concatenate([c_t[0], -c_t[1]], axis=2),
                      jnp.concatenate(w, axis=2), precision=hp)
    d_gh = d_skip.astype(F32).reshape(g_n, h_n)
    lag0 = kern[0][:, :, 0] + kern[1][:, :, 0] + jnp.eye(h_n, dtype=F32)[None] * d_gh[:, None, :]
    lags = jnp.concatenate([kern[1][:, :, :0:-1], lag0[:, :, None], kern[0][:, :, 1:],
                            jnp.zeros((g_n, h_n, 1, h_n), F32)], axis=2)
    lags = lags.reshape(g_n, h_n, 2 * c_len * h_n)
    toep = jnp.stack([lags[:, :, (c_len - 1 - s) * h_n:(2 * c_len - 1 - s) * h_n] for s in range(c_len)],
                     axis=1).reshape(g_n, CHUNK_COLS, CHUNK_COLS)

    bb_t = (jnp.swapaxes(b_bar[0], 2, 3), jnp.swapaxes(b_bar[1], 2, 3))
    as_t = (jnp.transpose(apow[0], (0, 2, 1, 3)), jnp.transpose(apow[1], (0, 2, 1, 3)))
    pw_f = (as_t[0][0, :, c_len - 1::-1][:, :, None, :], as_t[1][0, :, c_len - 1::-1][:, :, None, :])
    pw_b = (as_t[0][1, :, :c_len][:, :, None, :], as_t[1][1, :, :c_len][:, :, None, :])
    e_f = _cmul(pw_f, (bb_t[0][0][:, None], bb_t[1][0][:, None]))
    e_b = _cmul(pw_b, (bb_t[0][1][:, None], bb_t[1][1][:, None]))
    bm = jnp.stack([e_f[0], e_b[0], e_f[1], e_b[1]], axis=3).reshape(g_n, CHUNK_COLS, 4 * p_n)

    q_f = (ap_t[0][0, :, :, 1:][..., None], ap_t[1][0, :, :, 1:][..., None])
    q_b = (ap_t[0][1, :, :, :0:-1][..., None], ap_t[1][1, :, :, :0:-1][..., None])
    w_f = _cmul((c_t[0][0][:, :, None, :], c_t[1][0][:, :, None, :]), q_f)
    w_b = _cmul((c_t[0][1][:, :, None, :], c_t[1][1][:, :, None, :]), q_b)
    cm = jnp.stack([w_f[0], w_b[0], -w_f[1], -w_b[1]], axis=1).reshape(g_n, 4 * p_n, CHUNK_COLS)

    a16_re = jnp.concatenate([apow[0][0, c_len], apow[0][1, c_len]], axis=-1)[:, None, :]
    a16_im = jnp.concatenate([apow[1][0, c_len], apow[1][1, c_len]], axis=-1)[:, None, :]
    tcm = jnp.concatenate([toep, cm], axis=1)
    return bm.astype(BF16), tcm.astype(BF16), a16_re, a16_im


def _ssm_kernel(u_ref, bm_ref, tcm_ref, are_ref, aim_ref, y_ref, s_scr, xf_scr, xb_scr, *, nb, n_chunks):
    pitch = n_chunks + BATCH_PITCH_PAD
    ub = u_ref[0].astype(BF16)
    s = jnp.dot(ub, bm_ref[0], preferred_element_type=F32)
    for b in range(nb):
        for part in range(2):
            s_scr[part, b * pitch:b * pitch + n_chunks, :] = (
                s[b * n_chunks:(b + 1) * n_chunks, part * LANES:(part + 1) * LANES])
    a_re = jnp.broadcast_to(are_ref[0], (nb, LANES))
    a_im = jnp.broadcast_to(aim_ref[0], (nb, LANES))
    is_fwd = lax.broadcasted_iota(jnp.int32, (nb, LANES), 1) < SSM_STATE

    def step(i, carry):
        x_re, x_im = carry
        rf = pl.ds(i, nb, stride=pitch)
        rb = pl.ds(n_chunks - 1 - i, nb, stride=pitch)
        xf_scr[0, rf, :] = x_re
        xf_scr[1, rf, :] = x_im
        xb_scr[0, rb, :] = x_re
        xb_scr[1, rb, :] = x_im
        s_re = jnp.where(is_fwd, s_scr[0, rf, :], s_scr[0, rb, :])
        s_im = jnp.where(is_fwd, s_scr[1, rf, :], s_scr[1, rb, :])
        n_re = a_re * x_re - a_im * x_im + s_re
        n_im = a_re * x_im + a_im * x_re + s_im
        return n_re, n_im

    zero = jnp.zeros((nb, LANES), F32)
    lax.fori_loop(0, n_chunks, step, (zero, zero), unroll=8)
    fwd_rows = lax.broadcasted_iota(jnp.int32, (n_chunks, LANES), 1) < SSM_STATE
    x_in = []
    for b in range(nb):
        rows = slice(b * pitch, b * pitch + n_chunks)
        x_in.append(jnp.concatenate(
            [jnp.where(fwd_rows, xf_scr[part, rows, :], xb_scr[part, rows, :]) for part in range(2)], axis=1))
    x_in = jnp.concatenate(x_in, axis=0).astype(BF16)
    y = jnp.dot(ub, tcm_ref[0, :CHUNK_COLS, :], preferred_element_type=F32)
    y = y + jnp.dot(x_in, tcm_ref[0, CHUNK_COLS:, :], preferred_element_type=F32)
    y_ref[0] = y


def _ssm(u_g, bm, tcm, a16_re, a16_im, nb):
    g_n, rows, _ = u_g.shape
    n_chunks = rows // nb
    per_g = lambda g: (g, 0, 0)
    state_scr = pltpu.VMEM((2, nb * (n_chunks + BATCH_PITCH_PAD), LANES), F32)
    return pl.pallas_call(
        functools.partial(_ssm_kernel, nb=nb, n_chunks=n_chunks),
        grid=(g_n,),
        in_specs=[pl.BlockSpec((1, rows, CHUNK_COLS), per_g),
                  pl.BlockSpec((1, CHUNK_COLS, 4 * SSM_STATE), per_g),
                  pl.BlockSpec((1, CHUNK_COLS + 4 * SSM_STATE, CHUNK_COLS), per_g),
                  pl.BlockSpec((1, 1, LANES), per_g),
                  pl.BlockSpec((1, 1, LANES), per_g)],
        out_specs=pl.BlockSpec((1, rows, CHUNK_COLS), per_g),
        out_shape=jax.ShapeDtypeStruct((g_n, rows, CHUNK_COLS), F32),
        scratch_shapes=[state_scr, state_scr, state_scr],
        compiler_params=pltpu.CompilerParams(vmem_limit_bytes=VMEM_LIMIT),
        name="ssm_chunked",
    )(u_g, bm, tcm, a16_re, a16_im)


def _attn_bias_table(rpb):
    wh = NA_WIN_ROWS
    cols = jnp.arange(GRID_W)
    col_start = jnp.clip(cols - NA_WIN_COLS // 2, 0, GRID_W - NA_WIN_COLS)
    kc = jnp.arange(GRID_W)
    valid = (kc[None, :] >= col_start[:, None]) & (kc[None, :] < col_start[:, None] + NA_WIN_COLS)
    dc = kc[None, :] - cols[:, None] + NA_WIN_COLS - 1
    case = jnp.arange(wh)
    i = jnp.arange(wh)
    dr = i[None, :] - case[:, None] + NA_WIN_ROWS - 1
    n_dc = 2 * NA_WIN_COLS - 1
    hp = lax.Precision.HIGHEST
    rpb_ext = jnp.concatenate([rpb.astype(F32), jnp.full(rpb.shape[:2] + (1,), MASK_VALUE, F32)], axis=-1)
    rpb_ext = rpb_ext.reshape(N_HEAD_BLOCKS, HEADS_PER_BLOCK, 2 * NA_WIN_ROWS - 1, n_dc + 1)
    sel_r = (dr[:, :, None] == jnp.arange(2 * NA_WIN_ROWS - 1)).astype(F32)
    sel_c = (jnp.where(valid, dc, n_dc)[:, :, None] == jnp.arange(n_dc + 1)).astype(F32)
    by_col = jnp.einsum("bhrd,ckd->bhrck", rpb_ext, sel_c, precision=hp)
    tab = jnp.einsum("air,bhrck->bahcik", sel_r, by_col, precision=hp)
    return tab.reshape(N_HEAD_BLOCKS, wh, HEADS_PER_BLOCK * GRID_W, wh * GRID_W)


def _attn_kernel(q_ref, k_ref, v_ref, bias_ref, o_ref, *, rows_per_step, n_rows):
    wh = min(NA_WIN_ROWS, n_rows)
    lane = lax.broadcasted_iota(jnp.int32, (GRID_W, LANES), 1)
    first_head = lane < HEAD_DIM
    zero = jnp.zeros((GRID_W, LANES), BF16)
    kv_rows, scores = [], []
    for j in range(rows_per_step):
        r = pl.program_id(2) * rows_per_step + j
        rs = jnp.clip(r - wh // 2, 0, n_rows - wh)
        q = q_ref[0, pl.ds(pl.multiple_of(r * GRID_W, GRID_W), GRID_W), :]
        qs = jnp.concatenate([jnp.where(first_head, q, zero), jnp.where(first_head, zero, q)], axis=0)
        kv_rows.append(pl.ds(pl.multiple_of(rs * GRID_W, GRID_W), wh * GRID_W))
        kb = k_ref[0, kv_rows[j], :]
        s = lax.dot_general(qs, kb, (((1,), (1,)), ((), ())), preferred_element_type=F32)
        scores.append(s + bias_ref[0, r - rs])
    probs = [jnp.exp(s - jnp.max(s, axis=-1, keepdims=True)).astype(BF16) for s in scores]
    ones = jnp.ones((wh * GRID_W, LANES), BF16)
    for j in range(rows_per_step):
        v_aug = jnp.concatenate([v_ref[0, kv_rows[j], :], ones], axis=1)
        o = jnp.dot(probs[j], v_aug, preferred_element_type=F32)
        o = o[:, :LANES] / o[:, LANES:]
        o_ref[0, j * GRID_W:(j + 1) * GRID_W, :] = jnp.where(first_head, o[:GRID_W], o[GRID_W:]).astype(BF16)


def _attention(qkv, bias_tab, rows_per_step=8):
    nb, seq, _ = qkv.shape
    n_rows = seq // GRID_W
    wh = min(NA_WIN_ROWS, n_rows)
    assert wh == NA_WIN_ROWS and n_rows % rows_per_step == 0
    nhb = N_HEAD_BLOCKS
    return pl.pallas_call(
        functools.partial(_attn_kernel, rows_per_step=rows_per_step, n_rows=n_rows),
        grid=(nhb, nb, n_rows // rows_per_step),
        in_specs=[pl.BlockSpec((1, seq, LANES), lambda h, b, r: (b, 0, h)),
                  pl.BlockSpec((1, seq, LANES), lambda h, b, r: (b, 0, nhb + h)),
                  pl.BlockSpec((1, seq, LANES), lambda h, b, r: (b, 0, 2 * nhb + h)),
                  pl.BlockSpec((1, wh, HEADS_PER_BLOCK * GRID_W, wh * GRID_W), lambda h, b, r: (h, 0, 0, 0))],
        out_specs=pl.BlockSpec((1, rows_per_step * GRID_W, LANES), lambda h, b, r: (b, r, h)),
        out_shape=jax.ShapeDtypeStruct((nb, seq, ATTN_WIDTH), BF16),
        compiler_params=pltpu.CompilerParams(vmem_limit_bytes=VMEM_LIMIT),
        name="nbr_attention",
    )(qkv, qkv, qkv, bias_tab)


def _gelu_tanh(x):
    return 0.5 * x * (1.0 + jnp.tanh(math.sqrt(2.0 / math.pi) * (x + 0.044715 * (x * x * x))))


def _mix_kernel(x_ref, yg_ref, ya_ref, g_ref, wglu_ref, bglu_ref, ns_ref, na_ref, wout_ref, o_ref, y_scr):
    n_chunks = yg_ref.shape[1]
    for lb in range(SSM_WIDTH // LANES):
        for th in range(2):
            ys = [yg_ref[lb * PIECES + gl, :, th * LANES:(th + 1) * LANES] for gl in range(PIECES)]
            xs = _piece_transpose(ys)
            for i in range(PIECES):
                y_scr[lb, pl.ds(th * PIECES + i, n_chunks, stride=TOK_PITCH), :] = xs[i]
    y_tok = jnp.concatenate(
        [jnp.concatenate([y_scr[lb, c * TOK_PITCH:c * TOK_PITCH + SSM_CHUNK, :] for c in range(n_chunks)], axis=0)
         for lb in range(SSM_WIDTH // LANES)], axis=1)
    ys = _gelu_tanh(y_tok)
    z = jnp.dot(ys.astype(BF16), wglu_ref[...], preferred_element_type=F32) + bglu_ref[...]
    ys = ys * jax.nn.sigmoid(z)
    m_s = _rms(ys, ns_ref[...]).astype(BF16)
    m_a = _rms(ya_ref[0].astype(F32), na_ref[...]).astype(BF16)
    o = jnp.dot(m_s, wout_ref[:SSM_WIDTH, :], preferred_element_type=F32)
    o = o + jnp.dot(m_a, wout_ref[SSM_WIDTH:, :], preferred_element_type=F32)
    o_ref[0] = x_ref[0] + g_ref[0] * o


def _mix_out(x, y_g, y_att, g_mix, w_glu, b_glu, n_ssm, n_att, w_out, tm=512):
    nb, seq, _ = x.shape
    steps = seq // tm
    blk_chunks = tm // SSM_CHUNK
    tok = lambda w: pl.BlockSpec((1, tm, w), lambda b, i: (b, i, 0))
    const = lambda shape: pl.BlockSpec(shape, lambda b, i: (0,) * len(shape))
    return pl.pallas_call(
        _mix_kernel,
        grid=(nb, steps),
        in_specs=[tok(D_MODEL),
                  pl.BlockSpec((N_SSM_GROUPS, blk_chunks, CHUNK_COLS), lambda b, i: (0, b * steps + i, 0)),
                  tok(ATTN_WIDTH),
                  pl.BlockSpec((1, 1, D_MODEL), lambda b, i: (b, 0, 0)),
                  const((SSM_WIDTH, SSM_WIDTH)), const((1, SSM_WIDTH)),
                  const((1, SSM_WIDTH)), const((1, ATTN_WIDTH)),
                  const((D_MODEL, D_MODEL))],
        out_specs=tok(D_MODEL),
        out_shape=jax.ShapeDtypeStruct((nb, seq, D_MODEL), F32),
        scratch_shapes=[pltpu.VMEM((SSM_WIDTH // LANES, blk_chunks * TOK_PITCH, LANES), F32)],
        compiler_params=pltpu.CompilerParams(vmem_limit_bytes=VMEM_LIMIT),
        name="mix_out",
    )(x, y_g, y_att, g_mix, w_glu, b_glu, n_ssm, n_att, w_out)


def _ffn_kernel(x_ref, sh_ref, sc_ref, g_ref, nf_ref, wg_ref, wu_ref, wd_ref, nfin_ref, o_ref):
    x = x_ref[0]
    h = (_rms(x, nf_ref[...]) * (1.0 + sc_ref[0]) + sh_ref[0]).astype(BF16)
    f = jnp.zeros(x.shape, F32)
    for j in range(D_FF // FF_CHUNK):
        cols = slice(j * FF_CHUNK, (j + 1) * FF_CHUNK)
        gate = jnp.dot(h, wg_ref[:, cols], preferred_element_type=F32)
        up = jnp.dot(h, wu_ref[:, cols], preferred_element_type=F32)
        act = (gate * jax.nn.sigmoid(gate) * up).astype(BF16)
        f = f + jnp.dot(act, wd_ref[cols, :], preferred_element_type=F32)
    o_ref[0] = _rms(x + g_ref[0] * f, nfin_ref[...])


def _ffn(x, shift, scale, gate, n_ffn, w_gate, w_up, w_down, n_final, tm=512):
    nb, seq, _ = x.shape
    tok = pl.BlockSpec((1, tm, D_MODEL), lambda b, i: (b, i, 0))
    vec = pl.BlockSpec((1, 1, D_MODEL), lambda b, i: (b, 0, 0))
    const = lambda shape: pl.BlockSpec(shape, lambda b, i: (0,) * len(shape),
                                       pipeline_mode=pl.Buffered(1))
    return pl.pallas_call(
        _ffn_kernel,
        grid=(nb, seq // tm),
        in_specs=[tok, vec, vec, vec, const((1, D_MODEL)),
                  const((D_MODEL, D_FF)), const((D_MODEL, D_FF)), const((D_FF, D_MODEL)),
                  const((1, D_MODEL))],
        out_specs=tok,
        out_shape=jax.ShapeDtypeStruct((nb, seq, D_MODEL), F32),
        compiler_params=pltpu.CompilerParams(vmem_limit_bytes=VMEM_LIMIT),
        name="ffn_final",
    )(x, shift, scale, gate, n_ffn, w_gate, w_up, w_down, n_final)


def _trunk(x, mod, p):
    nb = x.shape[0]
    sh_mix, sc_mix, g_mix, sh_ffn, sc_ffn, g_ffn = [
        mod[:, i * D_MODEL:(i + 1) * D_MODEL].reshape(nb, 1, D_MODEL) for i in range(N_MOD)]
    u_g, qkv = _in_proj(x, sh_mix, sc_mix, p["norm_mix"], p["w_in"])
    y_g = _ssm(u_g, p["bm"], p["tcm"], p["a16_re"], p["a16_im"], nb)
    y_att = _attention(qkv, p["bias_tab"])
    x1 = _mix_out(x, y_g, y_att, g_mix, p["w_glu"], p["b_glu"], p["norm_ssm_out"],
                  p["norm_attn_out"], p["w_out"])
    return _ffn(x1, sh_ffn, sc_ffn, g_ffn, p["norm_ffn"], p["w_ffn_gate"], p["w_ffn_up"],
                p["w_ffn_down"], p["norm_final"])


def kernel(x_prompt, x_sample, c_prompt, c_sample, w_ada, b_ada, norm_mix, w_in, ssm_a_re, ssm_a_im, ssm_log_dt, ssm_b_re, ssm_b_im, ssm_c_re, ssm_c_im, ssm_d, w_glu, b_glu, norm_ssm_out, na_rpb, norm_attn_out, w_out, norm_ffn, w_ffn_gate, w_ffn_up, w_ffn_down, norm_final):
    assert w_ada.shape[0] == 1, "single-layer trunk"
    row = lambda v: v.reshape(1, -1).astype(F32)
    bm, tcm, a16_re, a16_im = _ssm_matrices(ssm_a_re[0], ssm_a_im[0], ssm_log_dt[0], ssm_b_re[0],
                                            ssm_b_im[0], ssm_c_re[0], ssm_c_im[0], ssm_d[0])
    p = dict(
        norm_mix=row(norm_mix[0]), w_in=w_in[0].astype(BF16),
        bm=bm, tcm=tcm, a16_re=a16_re, a16_im=a16_im,
        bias_tab=_attn_bias_table(na_rpb[0]),
        w_glu=w_glu[0].astype(BF16), b_glu=row(b_glu[0]),
        norm_ssm_out=row(norm_ssm_out[0]), norm_attn_out=row(norm_attn_out[0]),
        w_out=w_out[0].astype(BF16), norm_ffn=row(norm_ffn[0]),
        w_ffn_gate=w_ffn_gate[0].astype(BF16), w_ffn_up=w_ffn_up[0].astype(BF16),
        w_ffn_down=w_ffn_down[0].astype(BF16), norm_final=row(norm_final),
    )
    n_prompt = x_prompt.shape[0]
    mod = _ada_mod(jnp.concatenate([c_prompt, c_sample], axis=0), w_ada[0], b_ada[0])
    y_prompt = _trunk(x_prompt, mod[:n_prompt], p)
    y_sample = _trunk(x_sample, mod[n_prompt:], p)
    return (y_prompt, y_sample)
```

```python
import functools
import math

import jax
import jax.numpy as jnp
from jax import lax
from jax.experimental import pallas as pl
from jax.experimental.pallas import tpu as pltpu

D_MODEL = 1024
GRID_W = 64
SSM_WIDTH = D_MODEL // 2
SSM_GROUP_CH = 16
N_SSM_GROUPS = SSM_WIDTH // SSM_GROUP_CH
SSM_STATE = 64
ATTN_WIDTH = D_MODEL - SSM_WIDTH
HEAD_DIM = 64
N_HEADS_ATTN = ATTN_WIDTH // HEAD_DIM
IN_PROJ_WIDTH = SSM_WIDTH + 3 * ATTN_WIDTH
NA_WIN_ROWS = 8
NA_WIN_COLS = 16
D_FF = -(-8 * D_MODEL // (3 * 256)) * 256
N_MOD = 6
EPS = 1e-6

LANES = 128
SSM_CHUNK = 16
CHUNK_COLS = SSM_CHUNK * SSM_GROUP_CH
PIECES = LANES // SSM_GROUP_CH
TOK_PITCH = 24
BATCH_PITCH_PAD = 8
HEADS_PER_BLOCK = LANES // HEAD_DIM
N_HEAD_BLOCKS = N_HEADS_ATTN // HEADS_PER_BLOCK
FF_CHUNK = 256
MIX_SUBTILES = 2
MASK_VALUE = -1e30
VMEM_LIMIT = 56 * 1024 * 1024

F32 = jnp.float32
BF16 = jnp.bfloat16


def _rms(x, gain):
    return x * lax.rsqrt(jnp.mean(x * x, axis=-1, keepdims=True) + EPS) * gain


def _ada_kernel(c_ref, w_ref, b_ref, o_ref):
    c = c_ref[...]
    s = c * jax.nn.sigmoid(c)
    o_ref[...] = jnp.dot(s, w_ref[...], preferred_element_type=F32,
                         precision=lax.Precision.HIGHEST) + b_ref[...]


def _ada_mod(c, w_ada, b_ada):
    nb = c.shape[0]
    n_out = w_ada.shape[1]
    tn = 1536
    return pl.pallas_call(
        _ada_kernel,
        grid=(n_out // tn,),
        in_specs=[pl.BlockSpec((nb, D_MODEL), lambda j: (0, 0)),
                  pl.BlockSpec((D_MODEL, tn), lambda j: (0, j)),
                  pl.BlockSpec((1, tn), lambda j: (0, j))],
        out_specs=pl.BlockSpec((nb, tn), lambda j: (0, j)),
        out_shape=jax.ShapeDtypeStruct((nb, n_out), F32),
        name="ada_mod",
    )(c, w_ada, b_ada.reshape(1, n_out))


def _piece_transpose(xs):
    assert len(xs) == PIECES
    piece = lax.broadcasted_iota(jnp.int32, xs[0].shape, 1) // SSM_GROUP_CH
    xs = list(xs)
    for d in (4, 2, 1):
        keep = (piece & d) == 0
        nxt = list(xs)
        for a in range(len(xs)):
            if a & d:
                continue
            b = a + d
            nxt[a] = jnp.where(keep, xs[a], pltpu.roll(xs[b], d * SSM_GROUP_CH, axis=1))
            nxt[b] = jnp.where(keep, pltpu.roll(xs[a], LANES - d * SSM_GROUP_CH, axis=1), xs[b])
        xs = nxt
    return xs


def _inproj_kernel(x_ref, sh_ref, sc_ref, g_ref, w_ref, ug_ref, qkv_ref, u_scr):
    x = x_ref[0]
    h = _rms(x, g_ref[...]) * (1.0 + sc_ref[0]) + sh_ref[0]
    hb = h.astype(BF16)
    u = jnp.dot(hb, w_ref[:, :SSM_WIDTH], preferred_element_type=F32)
    n_chunks = u.shape[0] // SSM_CHUNK
    half = PIECES
    for lb in range(SSM_WIDTH // LANES):
        for c in range(n_chunks):
            u_scr[lb, c * TOK_PITCH:c * TOK_PITCH + SSM_CHUNK, :] = (
                u[c * SSM_CHUNK:(c + 1) * SSM_CHUNK, lb * LANES:(lb + 1) * LANES])
    for lb in range(SSM_WIDTH // LANES):
        for th in range(2):
            xs = [u_scr[lb, pl.ds(th * half + i, n_chunks, stride=TOK_PITCH), :] for i in range(half)]
            ys = _piece_transpose(xs)
            for gl in range(half):
                ug_ref[lb * half + gl, :, th * LANES:(th + 1) * LANES] = ys[gl]
    for j in range(3):
        lo = SSM_WIDTH + j * ATTN_WIDTH
        r = jnp.dot(hb, w_ref[:, lo:lo + ATTN_WIDTH], preferred_element_type=F32)
        if j == 0:
            r = r * (HEAD_DIM ** -0.5)
        qkv_ref[0, :, j * ATTN_WIDTH:(j + 1) * ATTN_WIDTH] = r.astype(BF16)


def _in_proj(x, shift, scale, gain, w_in_bf16, tm=512):
    nb, seq, _ = x.shape
    vec = pl.BlockSpec((1, 1, D_MODEL), lambda b, i: (b, 0, 0))
    steps = seq // tm
    blk_chunks = tm // SSM_CHUNK
    return pl.pallas_call(
        _inproj_kernel,
        grid=(nb, steps),
        in_specs=[pl.BlockSpec((1, tm, D_MODEL), lambda b, i: (b, i, 0)),
                  vec, vec,
                  pl.BlockSpec((1, D_MODEL), lambda b, i: (0, 0)),
                  pl.BlockSpec((D_MODEL, IN_PROJ_WIDTH), lambda b, i: (0, 0))],
        out_specs=[pl.BlockSpec((N_SSM_GROUPS, blk_chunks, CHUNK_COLS), lambda b, i: (0, b * steps + i, 0)),
                   pl.BlockSpec((1, tm, 3 * ATTN_WIDTH), lambda b, i: (b, i, 0))],
        out_shape=[jax.ShapeDtypeStruct((N_SSM_GROUPS, nb * seq // SSM_CHUNK, CHUNK_COLS), F32),
                   jax.ShapeDtypeStruct((nb, seq, 3 * ATTN_WIDTH), BF16)],
        scratch_shapes=[pltpu.VMEM((SSM_WIDTH // LANES, blk_chunks * TOK_PITCH, LANES), F32)],
        compiler_params=pltpu.CompilerParams(vmem_limit_bytes=VMEM_LIMIT),
        name="in_proj",
    )(x, shift, scale, gain, w_in_bf16)


def _cmul(a, b):
    return a[0] * b[0] - a[1] * b[1], a[0] * b[1] + a[1] * b[0]


def _ssm_matrices(a_re, a_im, log_dt, b_re, b_im, c_re, c_im, d_skip):
    hp = lax.Precision.HIGHEST
    c_len, g_n, p_n, h_n = SSM_CHUNK, N_SSM_GROUPS, SSM_STATE, SSM_GROUP_CH
    lam = (jnp.minimum(a_re.astype(F32), -1e-4), a_im.astype(F32))
    dt = jnp.exp(log_dt.astype(F32))[:, :, None]
    z = (lam[0] * dt, lam[1] * dt)
    j = jnp.arange(c_len + 1, dtype=F32)[None, :, None, None]
    mag = jnp.exp(z[0][:, None] * j)
    apow = (mag * jnp.cos(z[1][:, None] * j), mag * jnp.sin(z[1][:, None] * j))
    lam_sq = lam[0] * lam[0] + lam[1] * lam[1]
    zoh = _cmul((apow[0][:, 1] - 1.0, apow[1][:, 1]), (lam[0] / lam_sq, -lam[1] / lam_sq))
    b_bar = _cmul((zoh[0][..., None], zoh[1][..., None]), (b_re.astype(F32), b_im.astype(F32)))
    c = (c_re.astype(F32), c_im.astype(F32))
    c_t = (jnp.swapaxes(c[0], 2, 3), jnp.swapaxes(c[1], 2, 3))
    ap_t = (jnp.transpose(apow[0], (0, 2, 3, 1)), jnp.transpose(apow[1], (0, 2, 3, 1)))
    rep = lambda v: jnp.repeat(v, h_n, axis=-1)
    til = lambda v: jnp.tile(v, (1,) * (v.ndim - 1) + (c_len,))

    w = _cmul((rep(ap_t[0][..., :c_len]), rep(ap_t[1][..., :c_len])), (til(b_bar[0]), til(b_bar[1])))
    kt = jnp.einsum("dghk,dgkn->dghn", jnp.concatenate([c[0], -c[1]], axis=-1),
                    jnp.concatenate(w, axis=2), precision=hp)
    d_gh = d_skip.astype(F32).reshape(g_n, h_n)
    fwd_rev = kt[0].reshape(g_n, h_n, c_len, h_n)[:, :, :0:-1].reshape(g_n, h_n, (c_len - 1) * h_n)
    lag0 = kt[0][..., :h_n] + kt[1][..., :h_n] + jnp.eye(h_n, dtype=F32)[None] * d_gh[:, :, None]
    lags = jnp.concatenate([fwd_rev, lag0, kt[1][..., h_n:], jnp.zeros((g_n, h_n, h_n), F32)], axis=-1)

    bb_t = (jnp.swapaxes(b_bar[0], 2, 3), jnp.swapaxes(b_bar[1], 2, 3))
    as_t = (jnp.transpose(apow[0], (0, 2, 1, 3)), jnp.transpose(apow[1], (0, 2, 1, 3)))
    pw = tuple(jnp.concatenate([v[0, :, c_len - 1::-1], v[1, :, :c_len]], axis=-1)[:, :, None] for v in as_t)
    bb = tuple(jnp.concatenate([v[0], v[1]], axis=-1)[:, None] for v in bb_t)
    bm = jnp.concatenate(_cmul(pw, bb), axis=-1).reshape(g_n, CHUNK_COLS, 4 * p_n)

    c_cat = tuple(jnp.concatenate([v[0], v[1]], axis=1) for v in c_t)
    q_cat = tuple(jnp.concatenate([v[0, :, :, 1:], v[1, :, :, :0:-1]], axis=1) for v in ap_t)
    wq = _cmul((til(c_cat[0]), til(c_cat[1])), (rep(q_cat[0]), rep(q_cat[1])))
    cm = jnp.concatenate([wq[0], -wq[1]], axis=1)

    a16_re = jnp.concatenate([apow[0][0, c_len], apow[0][1, c_len]], axis=-1)[:, None, :]
    a16_im = jnp.concatenate([apow[1][0, c_len], apow[1][1, c_len]], axis=-1)[:, None, :]
    return bm.astype(BF16), cm.astype(BF16), lags, a16_re, a16_im


def _ssm_kernel(u_ref, bm_ref, cm_ref, lags_ref, are_ref, aim_ref, y_ref, s_scr, xf_scr, xb_scr, *, nb, n_chunks):
    pitch = n_chunks + BATCH_PITCH_PAD
    ub = u_ref[0].astype(BF16)
    s = jnp.dot(ub, bm_ref[0], preferred_element_type=F32)
    for b in range(nb):
        for part in range(2):
            s_scr[part, b * pitch:b * pitch + n_chunks, :] = (
                s[b * n_chunks:(b + 1) * n_chunks, part * LANES:(part + 1) * LANES])
    a_re = jnp.broadcast_to(are_ref[0], (nb, LANES))
    a_im = jnp.broadcast_to(aim_ref[0], (nb, LANES))
    is_fwd = lax.broadcasted_iota(jnp.int32, (nb, LANES), 1) < SSM_STATE

    def step(i, carry):
        x_re, x_im = carry
        rf = pl.ds(i, nb, stride=pitch)
        rb = pl.ds(n_chunks - 1 - i, nb, stride=pitch)
        xf_scr[0, rf, :] = x_re
        xf_scr[1, rf, :] = x_im
        xb_scr[0, rb, :] = x_re
        xb_scr[1, rb, :] = x_im
        s_re = jnp.where(is_fwd, s_scr[0, rf, :], s_scr[0, rb, :])
        s_im = jnp.where(is_fwd, s_scr[1, rf, :], s_scr[1, rb, :])
        n_re = a_re * x_re - a_im * x_im + s_re
        n_im = a_re * x_im + a_im * x_re + s_im
        return n_re, n_im

    zero = jnp.zeros((nb, LANES), F32)
    lax.fori_loop(0, n_chunks, step, (zero, zero), unroll=8)
    fwd_rows = lax.broadcasted_iota(jnp.int32, (n_chunks, LANES), 1) < SSM_STATE
    x_in = []
    for b in range(nb):
        rows = slice(b * pitch, b * pitch + n_chunks)
        x_in.append(jnp.concatenate(
            [jnp.where(fwd_rows, xf_scr[part, rows, :], xb_scr[part, rows, :]) for part in range(2)], axis=1))
    x_in = jnp.concatenate(x_in, axis=0).astype(BF16)
    lags = lags_ref[0]
    toep_t = jnp.concatenate(
        [lags[:, (SSM_CHUNK - 1 - t) * SSM_GROUP_CH:(SSM_CHUNK - 1 - t) * SSM_GROUP_CH + CHUNK_COLS]
         for t in range(SSM_CHUNK)], axis=0).astype(BF16)
    y = lax.dot_general(ub, toep_t, (((1,), (1,)), ((), ())), preferred_element_type=F32)
    y = y + jnp.dot(x_in, cm_ref[0], preferred_element_type=F32)
    y_ref[0] = y


def _ssm(u_g, bm, cm, lags, a16_re, a16_im, nb):
    g_n, rows, _ = u_g.shape
    n_chunks = rows // nb
    per_g = lambda g: (g, 0, 0)
    state_scr = pltpu.VMEM((2, nb * (n_chunks + BATCH_PITCH_PAD), LANES), F32)
    return pl.pallas_call(
        functools.partial(_ssm_kernel, nb=nb, n_chunks=n_chunks),
        grid=(g_n,),
        in_specs=[pl.BlockSpec((1, rows, CHUNK_COLS), per_g),
                  pl.BlockSpec((1, CHUNK_COLS, 4 * SSM_STATE), per_g),
                  pl.BlockSpec((1, 4 * SSM_STATE, CHUNK_COLS), per_g),
                  pl.BlockSpec((1, SSM_GROUP_CH, 2 * CHUNK_COLS), per_g),
                  pl.BlockSpec((1, 1, LANES), per_g),
                  pl.BlockSpec((1, 1, LANES), per_g)],
        out_specs=pl.BlockSpec((1, rows, CHUNK_COLS), per_g),
        out_shape=jax.ShapeDtypeStruct((g_n, rows, CHUNK_COLS), F32),
        scratch_shapes=[state_scr, state_scr, state_scr],
        compiler_params=pltpu.CompilerParams(vmem_limit_bytes=VMEM_LIMIT),
        name="ssm_chunked",
    )(u_g, bm, cm, lags, a16_re, a16_im)


def _attn_bias_table(rpb):
    wh = NA_WIN_ROWS
    cols = jnp.arange(GRID_W)
    col_start = jnp.clip(cols - NA_WIN_COLS // 2, 0, GRID_W - NA_WIN_COLS)
    kc = jnp.arange(GRID_W)
    valid = (kc[None, :] >= col_start[:, None]) & (kc[None, :] < col_start[:, None] + NA_WIN_COLS)
    dc = kc[None, :] - cols[:, None] + NA_WIN_COLS - 1
    case = jnp.arange(wh)
    i = jnp.arange(wh)
    dr = i[None, :] - case[:, None] + NA_WIN_ROWS - 1
    n_dc = 2 * NA_WIN_COLS - 1
    hp = lax.Precision.HIGHEST
    rpb_ext = jnp.concatenate([rpb.astype(F32), jnp.full(rpb.shape[:2] + (1,), MASK_VALUE, F32)], axis=-1)
    rpb_ext = rpb_ext.reshape(N_HEAD_BLOCKS, HEADS_PER_BLOCK, 2 * NA_WIN_ROWS - 1, n_dc + 1)
    sel_r = (dr[:, :, None] == jnp.arange(2 * NA_WIN_ROWS - 1)).astype(F32)
    sel_c = (jnp.where(valid, dc, n_dc)[:, :, None] == jnp.arange(n_dc + 1)).astype(F32)
    by_col = jnp.einsum("bhrd,ckd->bhrck", rpb_ext, sel_c, precision=hp)
    tab = jnp.einsum("air,bhrck->bahcik", sel_r, by_col, precision=hp)
    return tab.reshape(N_HEAD_BLOCKS, wh, HEADS_PER_BLOCK * GRID_W, wh * GRID_W)


def _attn_kernel(q_ref, k_ref, v_ref, bias_ref, o_ref, *, rows_per_step, n_rows):
    wh = min(NA_WIN_ROWS, n_rows)
    lane = lax.broadcasted_iota(jnp.int32, (GRID_W, LANES), 1)
    first_head = lane < HEAD_DIM
    zero = jnp.zeros((GRID_W, LANES), BF16)
    kv_rows, scores = [], []
    for j in range(rows_per_step):
        r = pl.program_id(2) * rows_per_step + j
        rs = jnp.clip(r - wh // 2, 0, n_rows - wh)
        q = q_ref[0, pl.ds(pl.multiple_of(r * GRID_W, GRID_W), GRID_W), :]
        qs = jnp.concatenate([jnp.where(first_head, q, zero), jnp.where(first_head, zero, q)], axis=0)
        kv_rows.append(pl.ds(pl.multiple_of(rs * GRID_W, GRID_W), wh * GRID_W))
        kb = k_ref[0, kv_rows[j], :]
        s = lax.dot_general(qs, kb, (((1,), (1,)), ((), ())), preferred_element_type=F32)
        scores.append(s + bias_ref[0, r - rs])
    probs = [jnp.exp(s - jnp.max(s, axis=-1, keepdims=True)).astype(BF16) for s in scores]
    ones = jnp.ones((wh * GRID_W, LANES), BF16)
    for j in range(rows_per_step):
        v_aug = jnp.concatenate([v_ref[0, kv_rows[j], :], ones], axis=1)
        o = jnp.dot(probs[j], v_aug, preferred_element_type=F32)
        o = o[:, :LANES] / o[:, LANES:]
        o_ref[0, j * GRID_W:(j + 1) * GRID_W, :] = jnp.where(first_head, o[:GRID_W], o[GRID_W:]).astype(BF16)


def _attention(qkv, bias_tab, rows_per_step=16):
    nb, seq, _ = qkv.shape
    n_rows = seq // GRID_W
    wh = min(NA_WIN_ROWS, n_rows)
    assert wh == NA_WIN_ROWS and n_rows % rows_per_step == 0
    nhb = N_HEAD_BLOCKS
    return pl.pallas_call(
        functools.partial(_attn_kernel, rows_per_step=rows_per_step, n_rows=n_rows),
        grid=(nhb, nb, n_rows // rows_per_step),
        in_specs=[pl.BlockSpec((1, seq, LANES), lambda h, b, r: (b, 0, h)),
                  pl.BlockSpec((1, seq, LANES), lambda h, b, r: (b, 0, nhb + h)),
                  pl.BlockSpec((1, seq, LANES), lambda h, b, r: (b, 0, 2 * nhb + h)),
                  pl.BlockSpec((1, wh, HEADS_PER_BLOCK * GRID_W, wh * GRID_W), lambda h, b, r: (h, 0, 0, 0))],
        out_specs=pl.BlockSpec((1, rows_per_step * GRID_W, LANES), lambda h, b, r: (b, r, h)),
        out_shape=jax.ShapeDtypeStruct((nb, seq, ATTN_WIDTH), BF16),
        compiler_params=pltpu.CompilerParams(vmem_limit_bytes=VMEM_LIMIT),
        name="nbr_attention",
    )(qkv, qkv, qkv, bias_tab)


def _gelu_tanh(x):
    return 0.5 * x * (1.0 + jnp.tanh(math.sqrt(2.0 / math.pi) * (x + 0.044715 * (x * x * x))))


def _mix_kernel(x_ref, yg_ref, ya_ref, g_ref, wglu_ref, bglu_ref, ns_ref, na_ref, wout_ref, o_ref, y_scr):
    n_chunks = yg_ref.shape[1]
    for lb in range(SSM_WIDTH // LANES):
        for th in range(2):
            ys = [yg_ref[lb * PIECES + gl, :, th * LANES:(th + 1) * LANES] for gl in range(PIECES)]
            xs = _piece_transpose(ys)
            for i in range(PIECES):
                y_scr[lb, pl.ds(th * PIECES + i, n_chunks, stride=TOK_PITCH), :] = xs[i]
    sub_chunks = n_chunks // MIX_SUBTILES
    sub = sub_chunks * SSM_CHUNK
    for k in range(MIX_SUBTILES):
        rows = slice(k * sub, (k + 1) * sub)
        y_tok = jnp.concatenate(
            [jnp.concatenate([y_scr[lb, c * TOK_PITCH:c * TOK_PITCH + SSM_CHUNK, :]
                              for c in range(k * sub_chunks, (k + 1) * sub_chunks)], axis=0)
             for lb in range(SSM_WIDTH // LANES)], axis=1)
        ys = _gelu_tanh(y_tok)
        z = jnp.dot(ys.astype(BF16), wglu_ref[...], preferred_element_type=F32) + bglu_ref[...]
        ys = ys * jax.nn.sigmoid(z)
        m_s = _rms(ys, ns_ref[...]).astype(BF16)
        m_a = _rms(ya_ref[0, rows, :].astype(F32), na_ref[...]).astype(BF16)
        o = jnp.dot(m_s, wout_ref[:SSM_WIDTH, :], preferred_element_type=F32)
        o = o + jnp.dot(m_a, wout_ref[SSM_WIDTH:, :], preferred_element_type=F32)
        o_ref[0, rows, :] = x_ref[0, rows, :] + g_ref[0] * o


def _mix_out(x, y_g, y_att, g_mix, w_glu, b_glu, n_ssm, n_att, w_out, tm=512):
    nb, seq, _ = x.shape
    steps = seq // tm
    blk_chunks = tm // SSM_CHUNK
    tok = lambda w: pl.BlockSpec((1, tm, w), lambda b, i: (b, i, 0))
    const = lambda shape: pl.BlockSpec(shape, lambda b, i: (0,) * len(shape))
    return pl.pallas_call(
        _mix_kernel,
        grid=(nb, steps),
        in_specs=[tok(D_MODEL),
                  pl.BlockSpec((N_SSM_GROUPS, blk_chunks, CHUNK_COLS), lambda b, i: (0, b * steps + i, 0)),
                  tok(ATTN_WIDTH),
                  pl.BlockSpec((1, 1, D_MODEL), lambda b, i: (b, 0, 0)),
                  const((SSM_WIDTH, SSM_WIDTH)), const((1, SSM_WIDTH)),
                  const((1, SSM_WIDTH)), const((1, ATTN_WIDTH)),
                  const((D_MODEL, D_MODEL))],
        out_specs=tok(D_MODEL),
        out_shape=jax.ShapeDtypeStruct((nb, seq, D_MODEL), F32),
        scratch_shapes=[pltpu.VMEM((SSM_WIDTH // LANES, blk_chunks * TOK_PITCH, LANES), F32)],
        compiler_params=pltpu.CompilerParams(vmem_limit_bytes=VMEM_LIMIT),
        name="mix_out",
    )(x, y_g, y_att, g_mix, w_glu, b_glu, n_ssm, n_att, w_out)


def _ffn_kernel(x_ref, sh_ref, sc_ref, g_ref, nf_ref, wg_ref, wu_ref, wd_ref, nfin_ref, o_ref):
    x = x_ref[0]
    h = (_rms(x, nf_ref[...]) * (1.0 + sc_ref[0]) + sh_ref[0]).astype(BF16)
    f = jnp.zeros(x.shape, F32)
    for j in range(D_FF // FF_CHUNK):
        cols = slice(j * FF_CHUNK, (j + 1) * FF_CHUNK)
        gate = jnp.dot(h, wg_ref[:, cols], preferred_element_type=F32)
        up = jnp.dot(h, wu_ref[:, cols], preferred_element_type=F32)
        act = (gate * jax.nn.sigmoid(gate) * up).astype(BF16)
        f = f + jnp.dot(act, wd_ref[cols, :], preferred_element_type=F32)
    o_ref[0] = _rms(x + g_ref[0] * f, nfin_ref[...])


def _ffn(x, shift, scale, gate, n_ffn, w_gate, w_up, w_down, n_final, tm=512):
    nb, seq, _ = x.shape
    tok = pl.BlockSpec((1, tm, D_MODEL), lambda b, i: (b, i, 0))
    vec = pl.BlockSpec((1, 1, D_MODEL), lambda b, i: (b, 0, 0))
    const = lambda shape: pl.BlockSpec(shape, lambda b, i: (0,) * len(shape),
                                       pipeline_mode=pl.Buffered(1))
    return pl.pallas_call(
        _ffn_kernel,
        grid=(nb, seq // tm),
        in_specs=[tok, vec, vec, vec, const((1, D_MODEL)),
                  const((D_MODEL, D_FF)), const((D_MODEL, D_FF)), const((D_FF, D_MODEL)),
                  const((1, D_MODEL))],
        out_specs=tok,
        out_shape=jax.ShapeDtypeStruct((nb, seq, D_MODEL), F32),
        compiler_params=pltpu.CompilerParams(vmem_limit_bytes=VMEM_LIMIT),
        name="ffn_final",
    )(x, shift, scale, gate, n_ffn, w_gate, w_up, w_down, n_final)


def _trunk(x, mod, p):
    nb = x.shape[0]
    sh_mix, sc_mix, g_mix, sh_ffn, sc_ffn, g_ffn = [
        mod[:, i * D_MODEL:(i + 1) * D_MODEL].reshape(nb, 1, D_MODEL) for i in range(N_MOD)]
    u_g, qkv = _in_proj(x, sh_mix, sc_mix, p["norm_mix"], p["w_in"])
    y_g = _ssm(u_g, p["bm"], p["cm"], p["lags"], p["a16_re"], p["a16_im"], nb)
    y_att = _attention(qkv, p["bias_tab"])
    x1 = _mix_out(x, y_g, y_att, g_mix, p["w_glu"], p["b_glu"], p["norm_ssm_out"],
                  p["norm_attn_out"], p["w_out"])
    return _ffn(x1, sh_ffn, sc_ffn, g_ffn, p["norm_ffn"], p["w_ffn_gate"], p["w_ffn_up"],
                p["w_ffn_down"], p["norm_final"])


def kernel(x_prompt, x_sample, c_prompt, c_sample, w_ada, b_ada, norm_mix, w_in, ssm_a_re, ssm_a_im, ssm_log_dt, ssm_b_re, ssm_b_im, ssm_c_re, ssm_c_im, ssm_d, w_glu, b_glu, norm_ssm_out, na_rpb, norm_attn_out, w_out, norm_ffn, w_ffn_gate, w_ffn_up, w_ffn_down, norm_final):
    assert w_ada.shape[0] == 1, "single-layer trunk"
    row = lambda v: v.reshape(1, -1).astype(F32)
    bm, cm, lags, a16_re, a16_im = _ssm_matrices(ssm_a_re[0], ssm_a_im[0], ssm_log_dt[0], ssm_b_re[0],
                                                 ssm_b_im[0], ssm_c_re[0], ssm_c_im[0], ssm_d[0])
    p = dict(
        norm_mix=row(norm_mix[0]), w_in=w_in[0].astype(BF16),
        bm=bm, cm=cm, lags=lags, a16_re=a16_re, a16_im=a16_im,
        bias_tab=_attn_bias_table(na_rpb[0]),
        w_glu=w_glu[0].astype(BF16), b_glu=row(b_glu[0]),
        norm_ssm_out=row(norm_ssm_out[0]), norm_attn_out=row(norm_attn_out[0]),
        w_out=w_out[0].astype(BF16), norm_ffn=row(norm_ffn[0]),
        w_ffn_gate=w_ffn_gate[0].astype(BF16), w_ffn_up=w_ffn_up[0].astype(BF16),
        w_ffn_down=w_ffn_down[0].astype(BF16), norm_final=row(norm_final),
    )
    n_prompt = x_prompt.shape[0]
    mod = _ada_mod(jnp.concatenate([c_prompt, c_sample], axis=0), w_ada[0], b_ada[0])
    y_prompt = _trunk(x_prompt, mod[:n_prompt], p)
    y_sample = _trunk(x_sample, mod[n_prompt:], p)
    return (y_prompt, y_sample)
```

```python
import functools
import math

import jax
import jax.numpy as jnp
from jax import lax
from jax.experimental import pallas as pl
from jax.experimental.pallas import tpu as pltpu

D_MODEL = 1024
GRID_W = 64
SSM_WIDTH = D_MODEL // 2
SSM_GROUP_CH = 16
N_SSM_GROUPS = SSM_WIDTH // SSM_GROUP_CH
SSM_STATE = 64
ATTN_WIDTH = D_MODEL - SSM_WIDTH
HEAD_DIM = 64
N_HEADS_ATTN = ATTN_WIDTH // HEAD_DIM
IN_PROJ_WIDTH = SSM_WIDTH + 3 * ATTN_WIDTH
NA_WIN_ROWS = 8
NA_WIN_COLS = 16
D_FF = -(-8 * D_MODEL // (3 * 256)) * 256
N_MOD = 6
EPS = 1e-6

LANES = 128
SSM_CHUNK = 16
CHUNK_COLS = SSM_CHUNK * SSM_GROUP_CH
PIECES = LANES // SSM_GROUP_CH
TOK_PITCH = 24
BATCH_PITCH_PAD = 8
HEADS_PER_BLOCK = LANES // HEAD_DIM
N_HEAD_BLOCKS = N_HEADS_ATTN // HEADS_PER_BLOCK
FF_CHUNK = 256
MIX_SUBTILES = 2
MASK_VALUE = -1e30
VMEM_LIMIT = 56 * 1024 * 1024

F32 = jnp.float32
BF16 = jnp.bfloat16


def _rms(x, gain):
    return x * lax.rsqrt(jnp.mean(x * x, axis=-1, keepdims=True) + EPS) * gain


def _ada_kernel(c_ref, w_ref, b_ref, o_ref):
    c = c_ref[...]
    s = c * jax.nn.sigmoid(c)
    o_ref[...] = jnp.dot(s, w_ref[...], preferred_element_type=F32,
                         precision=lax.Precision.HIGHEST) + b_ref[...]


def _ada_mod(c, w_ada, b_ada):
    nb = c.shape[0]
    n_out = w_ada.shape[1]
    tn = 1536
    return pl.pallas_call(
        _ada_kernel,
        grid=(n_out // tn,),
        in_specs=[pl.BlockSpec((nb, D_MODEL), lambda j: (0, 0)),
                  pl.BlockSpec((D_MODEL, tn), lambda j: (0, j)),
                  pl.BlockSpec((1, tn), lambda j: (0, j))],
        out_specs=pl.BlockSpec((nb, tn), lambda j: (0, j)),
        out_shape=jax.ShapeDtypeStruct((nb, n_out), F32),
        name="ada_mod",
    )(c, w_ada, b_ada.reshape(1, n_out))


def _piece_transpose(xs):
    assert len(xs) == PIECES
    piece = lax.broadcasted_iota(jnp.int32, xs[0].shape, 1) // SSM_GROUP_CH
    xs = list(xs)
    for d in (4, 2, 1):
        keep = (piece & d) == 0
        nxt = list(xs)
        for a in range(len(xs)):
            if a & d:
                continue
            b = a + d
            nxt[a] = jnp.where(keep, xs[a], pltpu.roll(xs[b], d * SSM_GROUP_CH, axis=1))
            nxt[b] = jnp.where(keep, pltpu.roll(xs[a], LANES - d * SSM_GROUP_CH, axis=1), xs[b])
        xs = nxt
    return xs


def _inproj_kernel(x_ref, sh_ref, sc_ref, g_ref, w_ref, ug_ref, qkv_ref, u_scr):
    x = x_ref[0]
    h = _rms(x, g_ref[...]) * (1.0 + sc_ref[0]) + sh_ref[0]
    hb = h.astype(BF16)
    u = jnp.dot(hb, w_ref[:, :SSM_WIDTH], preferred_element_type=F32)
    n_chunks = u.shape[0] // SSM_CHUNK
    half = PIECES
    for lb in range(SSM_WIDTH // LANES):
        for c in range(n_chunks):
            u_scr[lb, c * TOK_PITCH:c * TOK_PITCH + SSM_CHUNK, :] = (
                u[c * SSM_CHUNK:(c + 1) * SSM_CHUNK, lb * LANES:(lb + 1) * LANES])
    for lb in range(SSM_WIDTH // LANES):
        for th in range(2):
            xs = [u_scr[lb, pl.ds(th * half + i, n_chunks, stride=TOK_PITCH), :] for i in range(half)]
            ys = _piece_transpose(xs)
            for gl in range(half):
                ug_ref[lb * half + gl, :, th * LANES:(th + 1) * LANES] = ys[gl]
    for j in range(3):
        lo = SSM_WIDTH + j * ATTN_WIDTH
        r = jnp.dot(hb, w_ref[:, lo:lo + ATTN_WIDTH], preferred_element_type=F32)
        if j == 0:
            r = r * (HEAD_DIM ** -0.5)
        qkv_ref[0, :, j * ATTN_WIDTH:(j + 1) * ATTN_WIDTH] = r.astype(BF16)


def _in_proj(x, shift, scale, gain, w_in_bf16, tm=512):
    nb, seq, _ = x.shape
    vec = pl.BlockSpec((1, 1, D_MODEL), lambda b, i: (b, 0, 0))
    steps = seq // tm
    blk_chunks = tm // SSM_CHUNK
    return pl.pallas_call(
        _inproj_kernel,
        grid=(nb, steps),
        in_specs=[pl.BlockSpec((1, tm, D_MODEL), lambda b, i: (b, i, 0)),
                  vec, vec,
                  pl.BlockSpec((1, D_MODEL), lambda b, i: (0, 0)),
                  pl.BlockSpec((D_MODEL, IN_PROJ_WIDTH), lambda b, i: (0, 0))],
        out_specs=[pl.BlockSpec((N_SSM_GROUPS, blk_chunks, CHUNK_COLS), lambda b, i: (0, b * steps + i, 0)),
                   pl.BlockSpec((1, tm, 3 * ATTN_WIDTH), lambda b, i: (b, i, 0))],
        out_shape=[jax.ShapeDtypeStruct((N_SSM_GROUPS, nb * seq // SSM_CHUNK, CHUNK_COLS), F32),
                   jax.ShapeDtypeStruct((nb, seq, 3 * ATTN_WIDTH), BF16)],
        scratch_shapes=[pltpu.VMEM((SSM_WIDTH // LANES, blk_chunks * TOK_PITCH, LANES), F32)],
        compiler_params=pltpu.CompilerParams(vmem_limit_bytes=VMEM_LIMIT),
        name="in_proj",
    )(x, shift, scale, gain, w_in_bf16)


def _cmul(a, b):
    return a[0] * b[0] - a[1] * b[1], a[0] * b[1] + a[1] * b[0]


def _ssm_matrices(a_re, a_im, log_dt, b_re, b_im, c_re, c_im, d_skip):
    hp = lax.Precision.HIGHEST
    c_len, g_n, p_n, h_n = SSM_CHUNK, N_SSM_GROUPS, SSM_STATE, SSM_GROUP_CH
    lam = (jnp.minimum(a_re.astype(F32), -1e-4), a_im.astype(F32))
    dt = jnp.exp(log_dt.astype(F32))[:, :, None]
    z = (lam[0] * dt, lam[1] * dt)
    j = jnp.arange(c_len + 1, dtype=F32)[None, :, None, None]
    mag = jnp.exp(z[0][:, None] * j)
    apow = (mag * jnp.cos(z[1][:, None] * j), mag * jnp.sin(z[1][:, None] * j))
    lam_sq = lam[0] * lam[0] + lam[1] * lam[1]
    zoh = _cmul((apow[0][:, 1] - 1.0, apow[1][:, 1]), (lam[0] / lam_sq, -lam[1] / lam_sq))
    b_bar = _cmul((zoh[0][..., None], zoh[1][..., None]), (b_re.astype(F32), b_im.astype(F32)))
    c = (c_re.astype(F32), c_im.astype(F32))
    c_t = (jnp.swapaxes(c[0], 2, 3), jnp.swapaxes(c[1], 2, 3))
    ap_t = (jnp.transpose(apow[0], (0, 2, 3, 1)), jnp.transpose(apow[1], (0, 2, 3, 1)))
    rep = lambda v: jnp.repeat(v, h_n, axis=-1)
    til = lambda v: jnp.tile(v, (1,) * (v.ndim - 1) + (c_len,))

    w = _cmul((rep(ap_t[0][..., :c_len]), rep(ap_t[1][..., :c_len])), (til(b_bar[0]), til(b_bar[1])))
    kt = jnp.einsum("dghk,dgkn->dghn", jnp.concatenate([c[0], -c[1]], axis=-1),
                    jnp.concatenate(w, axis=2), precision=hp)
    d_gh = d_skip.astype(F32).reshape(g_n, h_n)
    fwd_rev = kt[0].reshape(g_n, h_n, c_len, h_n)[:, :, :0:-1].reshape(g_n, h_n, (c_len - 1) * h_n)
    lag0 = kt[0][..., :h_n] + kt[1][..., :h_n] + jnp.eye(h_n, dtype=F32)[None] * d_gh[:, :, None]
    lags = jnp.concatenate([fwd_rev, lag0, kt[1][..., h_n:], jnp.zeros((g_n, h_n, h_n), F32)], axis=-1)

    bb_t = (jnp.swapaxes(b_bar[0], 2, 3), jnp.swapaxes(b_bar[1], 2, 3))
    as_t = (jnp.transpose(apow[0], (0, 2, 1, 3)), jnp.transpose(apow[1], (0, 2, 1, 3)))
    pw = tuple(jnp.concatenate([v[0, :, c_len - 1::-1], v[1, :, :c_len]], axis=-1)[:, :, None] for v in as_t)
    bb = tuple(jnp.concatenate([v[0], v[1]], axis=-1)[:, None] for v in bb_t)
    bm = jnp.concatenate(_cmul(pw, bb), axis=-1).reshape(g_n, CHUNK_COLS, 4 * p_n)

    c_cat = tuple(jnp.concatenate([v[0], v[1]], axis=1) for v in c_t)
    q_cat = tuple(jnp.concatenate([v[0, :, :, 1:], v[1, :, :, :0:-1]], axis=1) for v in ap_t)
    wq = _cmul((til(c_cat[0]), til(c_cat[1])), (rep(q_cat[0]), rep(q_cat[1])))
    cm = jnp.concatenate([wq[0], -wq[1]], axis=1)

    a16_re = jnp.concatenate([apow[0][0, c_len], apow[0][1, c_len]], axis=-1)[:, None, :]
    a16_im = jnp.concatenate([apow[1][0, c_len], apow[1][1, c_len]], axis=-1)[:, None, :]
    return bm.astype(BF16), cm.astype(BF16), lags, a16_re, a16_im


def _ssm_kernel(u_ref, bm_ref, cm_ref, lags_ref, are_ref, aim_ref, y_ref, s_scr, xf_scr, xb_scr, *, nb, n_chunks):
    pitch = n_chunks + BATCH_PITCH_PAD
    ub = u_ref[0].astype(BF16)
    s = jnp.dot(ub, bm_ref[0], preferred_element_type=F32)
    for b in range(nb):
        for part in range(2):
            s_scr[part, b * pitch:b * pitch + n_chunks, :] = (
                s[b * n_chunks:(b + 1) * n_chunks, part * LANES:(part + 1) * LANES])
    a_re = jnp.broadcast_to(are_ref[0], (nb, LANES))
    a_im = jnp.broadcast_to(aim_ref[0], (nb, LANES))
    is_fwd = lax.broadcasted_iota(jnp.int32, (nb, LANES), 1) < SSM_STATE

    def step(i, carry):
        x_re, x_im = carry
        rf = pl.ds(i, nb, stride=pitch)
        rb = pl.ds(n_chunks - 1 - i, nb, stride=pitch)
        xf_scr[0, rf, :] = x_re
        xf_scr[1, rf, :] = x_im
        xb_scr[0, rb, :] = x_re
        xb_scr[1, rb, :] = x_im
        s_re = jnp.where(is_fwd, s_scr[0, rf, :], s_scr[0, rb, :])
        s_im = jnp.where(is_fwd, s_scr[1, rf, :], s_scr[1, rb, :])
        n_re = a_re * x_re - a_im * x_im + s_re
        n_im = a_re * x_im + a_im * x_re + s_im
        return n_re, n_im

    zero = jnp.zeros((nb, LANES), F32)
    lax.fori_loop(0, n_chunks, step, (zero, zero), unroll=8)
    fwd_rows = lax.broadcasted_iota(jnp.int32, (n_chunks, LANES), 1) < SSM_STATE
    x_in = []
    for b in range(nb):
        rows = slice(b * pitch, b * pitch + n_chunks)
        x_in.append(jnp.concatenate(
            [jnp.where(fwd_rows, xf_scr[part, rows, :], xb_scr[part, rows, :]) for part in range(2)], axis=1))
    x_in = jnp.concatenate(x_in, axis=0).astype(BF16)
    lags = lags_ref[0]
    toep_t = jnp.concatenate(
        [lags[:, (SSM_CHUNK - 1 - t) * SSM_GROUP_CH:(SSM_CHUNK - 1 - t) * SSM_GROUP_CH + CHUNK_COLS]
         for t in range(SSM_CHUNK)], axis=0).astype(BF16)
    y = lax.dot_general(ub, toep_t, (((1,), (1,)), ((), ())), preferred_element_type=F32)
    y = y + jnp.dot(x_in, cm_ref[0], preferred_element_type=F32)
    y_ref[0] = y


def _ssm(u_g, bm, cm, lags, a16_re, a16_im, nb):
    g_n, rows, _ = u_g.shape
    n_chunks = rows // nb
    per_g = lambda g: (g, 0, 0)
    state_scr = pltpu.VMEM((2, nb * (n_chunks + BATCH_PITCH_PAD), LANES), F32)
    return pl.pallas_call(
        functools.partial(_ssm_kernel, nb=nb, n_chunks=n_chunks),
        grid=(g_n,),
        in_specs=[pl.BlockSpec((1, rows, CHUNK_COLS), per_g),
                  pl.BlockSpec((1, CHUNK_COLS, 4 * SSM_STATE), per_g),
                  pl.BlockSpec((1, 4 * SSM_STATE, CHUNK_COLS), per_g),
                  pl.BlockSpec((1, SSM_GROUP_CH, 2 * CHUNK_COLS), per_g),
                  pl.BlockSpec((1, 1, LANES), per_g),
                  pl.BlockSpec((1, 1, LANES), per_g)],
        out_specs=pl.BlockSpec((1, rows, CHUNK_COLS), per_g),
        out_shape=jax.ShapeDtypeStruct((g_n, rows, CHUNK_COLS), F32),
        scratch_shapes=[state_scr, state_scr, state_scr],
        compiler_params=pltpu.CompilerParams(vmem_limit_bytes=VMEM_LIMIT),
        name="ssm_chunked",
    )(u_g, bm, cm, lags, a16_re, a16_im)


def _attn_bias_rows(rpb):
    cols = jnp.arange(GRID_W)
    col_start = jnp.clip(cols - NA_WIN_COLS // 2, 0, GRID_W - NA_WIN_COLS)
    kc = jnp.arange(GRID_W)
    valid = (kc[None, :] >= col_start[:, None]) & (kc[None, :] < col_start[:, None] + NA_WIN_COLS)
    dc = kc[None, :] - cols[:, None] + NA_WIN_COLS - 1
    n_dr, n_dc = 2 * NA_WIN_ROWS - 1, 2 * NA_WIN_COLS - 1
    rpb_ext = jnp.concatenate([rpb.astype(F32), jnp.full(rpb.shape[:2] + (1,), MASK_VALUE, F32)], axis=-1)
    rpb_ext = rpb_ext.reshape(N_HEAD_BLOCKS, HEADS_PER_BLOCK, n_dr, n_dc + 1)
    sel_c = (jnp.where(valid, dc, n_dc)[:, :, None] == jnp.arange(n_dc + 1)).astype(F32)
    by_col = jnp.einsum("bhrd,ckd->brhck", rpb_ext, sel_c, precision=lax.Precision.HIGHEST)
    return by_col.reshape(N_HEAD_BLOCKS, n_dr, HEADS_PER_BLOCK * GRID_W, GRID_W)


def _attn_kernel(q_ref, k_ref, v_ref, bias_ref, o_ref, tab_scr, *, rows_per_step, n_rows):
    wh = min(NA_WIN_ROWS, n_rows)

    @pl.when((pl.program_id(1) == 0) & (pl.program_id(2) == 0))
    def _():
        for case in range(wh):
            for pair in range(wh // 2):
                dr = NA_WIN_ROWS - 1 - case + 2 * pair
                tab_scr[case, :, pair * LANES:(pair + 1) * LANES] = jnp.concatenate(
                    [bias_ref[0, dr], bias_ref[0, dr + 1]], axis=1)

    lane = lax.broadcasted_iota(jnp.int32, (GRID_W, LANES), 1)
    first_head = lane < HEAD_DIM
    zero = jnp.zeros((GRID_W, LANES), BF16)
    kv_rows, scores = [], []
    for j in range(rows_per_step):
        r = pl.program_id(2) * rows_per_step + j
        rs = jnp.clip(r - wh // 2, 0, n_rows - wh)
        q = q_ref[0, pl.ds(pl.multiple_of(r * GRID_W, GRID_W), GRID_W), :]
        qs = jnp.concatenate([jnp.where(first_head, q, zero), jnp.where(first_head, zero, q)], axis=0)
        kv_rows.append(pl.ds(pl.multiple_of(rs * GRID_W, GRID_W), wh * GRID_W))
        kb = k_ref[0, kv_rows[j], :]
        s = lax.dot_general(qs, kb, (((1,), (1,)), ((), ())), preferred_element_type=F32)
        scores.append(s + tab_scr[r - rs])
    probs = [jnp.exp(s - jnp.max(s, axis=-1, keepdims=True)).astype(BF16) for s in scores]
    ones = jnp.ones((wh * GRID_W, LANES), BF16)
    for j in range(rows_per_step):
        v_aug = jnp.concatenate([v_ref[0, kv_rows[j], :], ones], axis=1)
        o = jnp.dot(probs[j], v_aug, preferred_element_type=F32)
        o = o[:, :LANES] / o[:, LANES:]
        o_ref[0, j * GRID_W:(j + 1) * GRID_W, :] = jnp.where(first_head, o[:GRID_W], o[GRID_W:]).astype(BF16)


def _attention(qkv, bias_rows, rows_per_step=16):
    nb, seq, _ = qkv.shape
    n_rows = seq // GRID_W
    wh = min(NA_WIN_ROWS, n_rows)
    assert wh == NA_WIN_ROWS and n_rows % rows_per_step == 0
    nhb = N_HEAD_BLOCKS
    return pl.pallas_call(
        functools.partial(_attn_kernel, rows_per_step=rows_per_step, n_rows=n_rows),
        grid=(nhb, nb, n_rows // rows_per_step),
        in_specs=[pl.BlockSpec((1, seq, LANES), lambda h, b, r: (b, 0, h)),
                  pl.BlockSpec((1, seq, LANES), lambda h, b, r: (b, 0, nhb + h)),
                  pl.BlockSpec((1, seq, LANES), lambda h, b, r: (b, 0, 2 * nhb + h)),
                  pl.BlockSpec((1, 2 * NA_WIN_ROWS - 1, HEADS_PER_BLOCK * GRID_W, GRID_W),
                               lambda h, b, r: (h, 0, 0, 0))],
        out_specs=pl.BlockSpec((1, rows_per_step * GRID_W, LANES), lambda h, b, r: (b, r, h)),
        out_shape=jax.ShapeDtypeStruct((nb, seq, ATTN_WIDTH), BF16),
        scratch_shapes=[pltpu.VMEM((wh, HEADS_PER_BLOCK * GRID_W, wh * GRID_W), F32)],
        compiler_params=pltpu.CompilerParams(vmem_limit_bytes=VMEM_LIMIT),
        name="nbr_attention",
    )(qkv, qkv, qkv, bias_rows)


def _gelu_tanh(x):
    return 0.5 * x * (1.0 + jnp.tanh(math.sqrt(2.0 / math.pi) * (x + 0.044715 * (x * x * x))))


def _mix_ffn_kernel(x_ref, yg_ref, ya_ref, gm_ref, sh_ref, sc_ref, gf_ref, wglu_ref, bglu_ref, ns_ref, na_ref,
                    wout_ref, nf_ref, wg_ref, wu_ref, wd_ref, nfin_ref, o_ref, y_scr, x1_scr):
    n_chunks = yg_ref.shape[1]
    for lb in range(SSM_WIDTH // LANES):
        for th in range(2):
            ys = [yg_ref[lb * PIECES + gl, :, th * LANES:(th + 1) * LANES] for gl in range(PIECES)]
            xs = _piece_transpose(ys)
            for i in range(PIECES):
                y_scr[lb, pl.ds(th * PIECES + i, n_chunks, stride=TOK_PITCH), :] = xs[i]
    sub_chunks = n_chunks // MIX_SUBTILES
    sub = sub_chunks * SSM_CHUNK
    for k in range(MIX_SUBTILES):
        rows = slice(k * sub, (k + 1) * sub)
        y_tok = jnp.concatenate(
            [jnp.concatenate([y_scr[lb, c * TOK_PITCH:c * TOK_PITCH + SSM_CHUNK, :]
                              for c in range(k * sub_chunks, (k + 1) * sub_chunks)], axis=0)
             for lb in range(SSM_WIDTH // LANES)], axis=1)
        ys = _gelu_tanh(y_tok)
        z = jnp.dot(ys.astype(BF16), wglu_ref[...], preferred_element_type=F32) + bglu_ref[...]
        ys = ys * jax.nn.sigmoid(z)
        m_s = _rms(ys, ns_ref[...]).astype(BF16)
        m_a = _rms(ya_ref[0, rows, :].astype(F32), na_ref[...]).astype(BF16)
        o = jnp.dot(m_s, wout_ref[:SSM_WIDTH, :], preferred_element_type=F32)
        o = o + jnp.dot(m_a, wout_ref[SSM_WIDTH:, :], preferred_element_type=F32)
        x1_scr[rows, :] = x_ref[0, rows, :] + gm_ref[0] * o
    x1 = x1_scr[...]
    h = (_rms(x1, nf_ref[...]) * (1.0 + sc_ref[0]) + sh_ref[0]).astype(BF16)
    f = jnp.zeros(x1.shape, F32)
    for j in range(D_FF // FF_CHUNK):
        cols = slice(j * FF_CHUNK, (j + 1) * FF_CHUNK)
        gate = jnp.dot(h, wg_ref[:, cols], preferred_element_type=F32)
        up = jnp.dot(h, wu_ref[:, cols], preferred_element_type=F32)
        act = (gate * jax.nn.sigmoid(gate) * up).astype(BF16)
        f = f + jnp.dot(act, wd_ref[cols, :], preferred_element_type=F32)
    o_ref[0] = _rms(x1 + gf_ref[0] * f, nfin_ref[...])


def _mix_ffn(x, y_g, y_att, g_mix, sh_ffn, sc_ffn, g_ffn, p, tm=512):
    nb, seq, _ = x.shape
    steps = seq // tm
    blk_chunks = tm // SSM_CHUNK
    tok = lambda w: pl.BlockSpec((1, tm, w), lambda b, i: (b, i, 0))
    vec = pl.BlockSpec((1, 1, D_MODEL), lambda b, i: (b, 0, 0))
    const = lambda shape: pl.BlockSpec(shape, lambda b, i: (0,) * len(shape), pipeline_mode=pl.Buffered(1))
    return pl.pallas_call(
        _mix_ffn_kernel,
        grid=(nb, steps),
        in_specs=[tok(D_MODEL),
                  pl.BlockSpec((N_SSM_GROUPS, blk_chunks, CHUNK_COLS), lambda b, i: (0, b * steps + i, 0)),
                  tok(ATTN_WIDTH), vec, vec, vec, vec,
                  const((SSM_WIDTH, SSM_WIDTH)), const((1, SSM_WIDTH)),
                  const((1, SSM_WIDTH)), const((1, ATTN_WIDTH)), const((D_MODEL, D_MODEL)),
                  const((1, D_MODEL)), const((D_MODEL, D_FF)), const((D_MODEL, D_FF)), const((D_FF, D_MODEL)),
                  const((1, D_MODEL))],
        out_specs=tok(D_MODEL),
        out_shape=jax.ShapeDtypeStruct((nb, seq, D_MODEL), F32),
        scratch_shapes=[pltpu.VMEM((SSM_WIDTH // LANES, blk_chunks * TOK_PITCH, LANES), F32),
                        pltpu.VMEM((tm, D_MODEL), F32)],
        compiler_params=pltpu.CompilerParams(vmem_limit_bytes=VMEM_LIMIT),
        name="mix_ffn",
    )(x, y_g, y_att, g_mix, sh_ffn, sc_ffn, g_ffn, p["w_glu"], p["b_glu"], p["norm_ssm_out"], p["norm_attn_out"],
      p["w_out"], p["norm_ffn"], p["w_ffn_gate"], p["w_ffn_up"], p["w_ffn_down"], p["norm_final"])


def _trunk(x, mod, p):
    nb = x.shape[0]
    sh_mix, sc_mix, g_mix, sh_ffn, sc_ffn, g_ffn = [
        mod[:, i * D_MODEL:(i + 1) * D_MODEL].reshape(nb, 1, D_MODEL) for i in range(N_MOD)]
    u_g, qkv = _in_proj(x, sh_mix, sc_mix, p["norm_mix"], p["w_in"])
    y_g = _ssm(u_g, p["bm"], p["cm"], p["lags"], p["a16_re"], p["a16_im"], nb)
    y_att = _attention(qkv, p["bias_rows"])
    return _mix_ffn(x, y_g, y_att, g_mix, sh_ffn, sc_ffn, g_ffn, p)


def kernel(x_prompt, x_sample, c_prompt, c_sample, w_ada, b_ada, norm_mix, w_in, ssm_a_re, ssm_a_im, ssm_log_dt, ssm_b_re, ssm_b_im, ssm_c_re, ssm_c_im, ssm_d, w_glu, b_glu, norm_ssm_out, na_rpb, norm_attn_out, w_out, norm_ffn, w_ffn_gate, w_ffn_up, w_ffn_down, norm_final):
    assert w_ada.shape[0] == 1, "single-layer trunk"
    row = lambda v: v.reshape(1, -1).astype(F32)
    bm, cm, lags, a16_re, a16_im = _ssm_matrices(ssm_a_re[0], ssm_a_im[0], ssm_log_dt[0], ssm_b_re[0],
                                                 ssm_b_im[0], ssm_c_re[0], ssm_c_im[0], ssm_d[0])
    p = dict(
        norm_mix=row(norm_mix[0]), w_in=w_in[0].astype(BF16),
        bm=bm, cm=cm, lags=lags, a16_re=a16_re, a16_im=a16_im,
        bias_rows=_attn_bias_rows(na_rpb[0]),
        w_glu=w_glu[0].astype(BF16), b_glu=row(b_glu[0]),
        norm_ssm_out=row(norm_ssm_out[0]), norm_attn_out=row(norm_attn_out[0]),
        w_out=w_out[0].astype(BF16), norm_ffn=row(norm_ffn[0]),
        w_ffn_gate=w_ffn_gate[0].astype(BF16), w_ffn_up=w_ffn_up[0].astype(BF16),
        w_ffn_down=w_ffn_down[0].astype(BF16), norm_final=row(norm_final),
    )
    n_prompt = x_prompt.shape[0]
    mod = _ada_mod(jnp.concatenate([c_prompt, c_sample], axis=0), w_ada[0], b_ada[0])
    y_prompt = _trunk(x_prompt, mod[:n_prompt], p)
    y_sample = _trunk(x_sample, mod[n_prompt:], p)
    return (y_prompt, y_sample)
```

```python
import functools
import math

import jax
import jax.numpy as jnp
from jax import lax
from jax.experimental import pallas as pl
from jax.experimental.pallas import tpu as pltpu

D_MODEL = 1024
GRID_W = 64
SSM_WIDTH = D_MODEL // 2
SSM_GROUP_CH = 16
N_SSM_GROUPS = SSM_WIDTH // SSM_GROUP_CH
SSM_STATE = 64
ATTN_WIDTH = D_MODEL - SSM_WIDTH
HEAD_DIM = 64
N_HEADS_ATTN = ATTN_WIDTH // HEAD_DIM
IN_PROJ_WIDTH = SSM_WIDTH + 3 * ATTN_WIDTH
NA_WIN_ROWS = 8
NA_WIN_COLS = 16
D_FF = -(-8 * D_MODEL // (3 * 256)) * 256
N_MOD = 6
EPS = 1e-6

LANES = 128
SSM_CHUNK = 16
CHUNK_COLS = SSM_CHUNK * SSM_GROUP_CH
PIECES = LANES // SSM_GROUP_CH
TOK_PITCH = 24
BATCH_PITCH_PAD = 8
HEADS_PER_BLOCK = LANES // HEAD_DIM
N_HEAD_BLOCKS = N_HEADS_ATTN // HEADS_PER_BLOCK
FF_CHUNK = 256
MIX_SUBTILES = 4
FFN_TILE = 512
MASK_VALUE = -1e30
VMEM_LIMIT = 56 * 1024 * 1024

F32 = jnp.float32
BF16 = jnp.bfloat16


def _rms(x, gain):
    return x * lax.rsqrt(jnp.mean(x * x, axis=-1, keepdims=True) + EPS) * gain


def _ada_kernel(c_ref, w_ref, b_ref, o_ref):
    c = c_ref[...]
    s = c * jax.nn.sigmoid(c)
    o_ref[...] = jnp.dot(s, w_ref[...], preferred_element_type=F32,
                         precision=lax.Precision.HIGHEST) + b_ref[...]


def _ada_mod(c, w_ada, b_ada):
    nb = c.shape[0]
    n_out = w_ada.shape[1]
    tn = 1536
    return pl.pallas_call(
        _ada_kernel,
        grid=(n_out // tn,),
        in_specs=[pl.BlockSpec((nb, D_MODEL), lambda j: (0, 0)),
                  pl.BlockSpec((D_MODEL, tn), lambda j: (0, j)),
                  pl.BlockSpec((1, tn), lambda j: (0, j))],
        out_specs=pl.BlockSpec((nb, tn), lambda j: (0, j)),
        out_shape=jax.ShapeDtypeStruct((nb, n_out), F32),
        name="ada_mod",
    )(c, w_ada, b_ada.reshape(1, n_out))


def _piece_transpose(xs):
    assert len(xs) == PIECES
    piece = lax.broadcasted_iota(jnp.int32, xs[0].shape, 1) // SSM_GROUP_CH
    xs = list(xs)
    for d in (4, 2, 1):
        keep = (piece & d) == 0
        nxt = list(xs)
        for a in range(len(xs)):
            if a & d:
                continue
            b = a + d
            nxt[a] = jnp.where(keep, xs[a], pltpu.roll(xs[b], d * SSM_GROUP_CH, axis=1))
            nxt[b] = jnp.where(keep, pltpu.roll(xs[a], LANES - d * SSM_GROUP_CH, axis=1), xs[b])
        xs = nxt
    return xs


def _inproj_kernel(x_ref, sh_ref, sc_ref, g_ref, w_ref, ug_ref, qkv_ref, u_scr):
    x = x_ref[0]
    h = _rms(x, g_ref[...]) * (1.0 + sc_ref[0]) + sh_ref[0]
    hb = h.astype(BF16)
    u = jnp.dot(hb, w_ref[:, :SSM_WIDTH], preferred_element_type=F32)
    n_chunks = u.shape[0] // SSM_CHUNK
    half = PIECES
    for lb in range(SSM_WIDTH // LANES):
        for c in range(n_chunks):
            u_scr[lb, c * TOK_PITCH:c * TOK_PITCH + SSM_CHUNK, :] = (
                u[c * SSM_CHUNK:(c + 1) * SSM_CHUNK, lb * LANES:(lb + 1) * LANES])
    for lb in range(SSM_WIDTH // LANES):
        for th in range(2):
            xs = [u_scr[lb, pl.ds(th * half + i, n_chunks, stride=TOK_PITCH), :] for i in range(half)]
            ys = _piece_transpose(xs)
            for gl in range(half):
                ug_ref[lb * half + gl, :, th * LANES:(th + 1) * LANES] = ys[gl]
    for j in range(3):
        lo = SSM_WIDTH + j * ATTN_WIDTH
        r = jnp.dot(hb, w_ref[:, lo:lo + ATTN_WIDTH], preferred_element_type=F32)
        if j == 0:
            r = r * (HEAD_DIM ** -0.5)
        qkv_ref[0, :, j * ATTN_WIDTH:(j + 1) * ATTN_WIDTH] = r.astype(BF16)


def _in_proj(x, shift, scale, gain, w_in_bf16, tm=512):
    nb, seq, _ = x.shape
    vec = pl.BlockSpec((1, 1, D_MODEL), lambda b, i: (b, 0, 0))
    steps = seq // tm
    blk_chunks = tm // SSM_CHUNK
    return pl.pallas_call(
        _inproj_kernel,
        grid=(nb, steps),
        in_specs=[pl.BlockSpec((1, tm, D_MODEL), lambda b, i: (b, i, 0)),
                  vec, vec,
                  pl.BlockSpec((1, D_MODEL), lambda b, i: (0, 0)),
                  pl.BlockSpec((D_MODEL, IN_PROJ_WIDTH), lambda b, i: (0, 0))],
        out_specs=[pl.BlockSpec((N_SSM_GROUPS, blk_chunks, CHUNK_COLS), lambda b, i: (0, b * steps + i, 0)),
                   pl.BlockSpec((1, tm, 3 * ATTN_WIDTH), lambda b, i: (b, i, 0))],
        out_shape=[jax.ShapeDtypeStruct((N_SSM_GROUPS, nb * seq // SSM_CHUNK, CHUNK_COLS), F32),
                   jax.ShapeDtypeStruct((nb, seq, 3 * ATTN_WIDTH), BF16)],
        scratch_shapes=[pltpu.VMEM((SSM_WIDTH // LANES, blk_chunks * TOK_PITCH, LANES), F32)],
        compiler_params=pltpu.CompilerParams(vmem_limit_bytes=VMEM_LIMIT),
        name="in_proj",
    )(x, shift, scale, gain, w_in_bf16)


def _cmul(a, b):
    return a[0] * b[0] - a[1] * b[1], a[0] * b[1] + a[1] * b[0]


def _ssm_matrices(a_re, a_im, log_dt, b_re, b_im, c_re, c_im, d_skip):
    hp = lax.Precision.HIGHEST
    c_len, g_n, p_n, h_n = SSM_CHUNK, N_SSM_GROUPS, SSM_STATE, SSM_GROUP_CH
    lam = (jnp.minimum(a_re.astype(F32), -1e-4), a_im.astype(F32))
    dt = jnp.exp(log_dt.astype(F32))[:, :, None]
    z = (lam[0] * dt, lam[1] * dt)
    j = jnp.arange(c_len + 1, dtype=F32)[None, :, None, None]
    mag = jnp.exp(z[0][:, None] * j)
    apow = (mag * jnp.cos(z[1][:, None] * j), mag * jnp.sin(z[1][:, None] * j))
    lam_sq = lam[0] * lam[0] + lam[1] * lam[1]
    zoh = _cmul((apow[0][:, 1] - 1.0, apow[1][:, 1]), (lam[0] / lam_sq, -lam[1] / lam_sq))
    b_bar = _cmul((zoh[0][..., None], zoh[1][..., None]), (b_re.astype(F32), b_im.astype(F32)))
    c = (c_re.astype(F32), c_im.astype(F32))
    c_t = (jnp.swapaxes(c[0], 2, 3), jnp.swapaxes(c[1], 2, 3))
    ap_t = (jnp.transpose(apow[0], (0, 2, 3, 1)), jnp.transpose(apow[1], (0, 2, 3, 1)))
    rep = lambda v: jnp.repeat(v, h_n, axis=-1)
    til = lambda v: jnp.tile(v, (1,) * (v.ndim - 1) + (c_len,))

    w = _cmul((rep(ap_t[0][..., :c_len]), rep(ap_t[1][..., :c_len])), (til(b_bar[0]), til(b_bar[1])))
    kt = jnp.einsum("dghk,dgkn->dghn", jnp.concatenate([c[0], -c[1]], axis=-1),
                    jnp.concatenate(w, axis=2), precision=hp)
    d_gh = d_skip.astype(F32).reshape(g_n, h_n)
    fwd_rev = kt[0].reshape(g_n, h_n, c_len, h_n)[:, :, :0:-1].reshape(g_n, h_n, (c_len - 1) * h_n)
    lag0 = kt[0][..., :h_n] + kt[1][..., :h_n] + jnp.eye(h_n, dtype=F32)[None] * d_gh[:, :, None]
    lags = jnp.concatenate([fwd_rev, lag0, kt[1][..., h_n:], jnp.zeros((g_n, h_n, h_n), F32)], axis=-1)

    bb_t = (jnp.swapaxes(b_bar[0], 2, 3), jnp.swapaxes(b_bar[1], 2, 3))
    as_t = (jnp.transpose(apow[0], (0, 2, 1, 3)), jnp.transpose(apow[1], (0, 2, 1, 3)))
    pw = tuple(jnp.concatenate([v[0, :, c_len - 1::-1], v[1, :, :c_len]], axis=-1)[:, :, None] for v in as_t)
    bb = tuple(jnp.concatenate([v[0], v[1]], axis=-1)[:, None] for v in bb_t)
    bm = jnp.concatenate(_cmul(pw, bb), axis=-1).reshape(g_n, CHUNK_COLS, 4 * p_n)

    c_cat = tuple(jnp.concatenate([v[0], v[1]], axis=1) for v in c_t)
    q_cat = tuple(jnp.concatenate([v[0, :, :, 1:], v[1, :, :, :0:-1]], axis=1) for v in ap_t)
    wq = _cmul((til(c_cat[0]), til(c_cat[1])), (rep(q_cat[0]), rep(q_cat[1])))
    cm = jnp.concatenate([wq[0], -wq[1]], axis=1)

    a16_re = jnp.concatenate([apow[0][0, c_len], apow[0][1, c_len]], axis=-1)[:, None, :]
    a16_im = jnp.concatenate([apow[1][0, c_len], apow[1][1, c_len]], axis=-1)[:, None, :]
    return bm.astype(BF16), cm.astype(BF16), lags, a16_re, a16_im


def _ssm_kernel(u_ref, bm_ref, cm_ref, lags_ref, are_ref, aim_ref, y_ref, s_scr, xf_scr, xb_scr, *, nb, n_chunks):
    pitch = n_chunks + BATCH_PITCH_PAD
    ub = u_ref[0].astype(BF16)
    s = jnp.dot(ub, bm_ref[0], preferred_element_type=F32)
    for b in range(nb):
        for part in range(2):
            s_scr[part, b * pitch:b * pitch + n_chunks, :] = (
                s[b * n_chunks:(b + 1) * n_chunks, part * LANES:(part + 1) * LANES])
    a_re = jnp.broadcast_to(are_ref[0], (nb, LANES))
    a_im = jnp.broadcast_to(aim_ref[0], (nb, LANES))
    is_fwd = lax.broadcasted_iota(jnp.int32, (nb, LANES), 1) < SSM_STATE

    def step(i, carry):
        x_re, x_im = carry
        rf = pl.ds(i, nb, stride=pitch)
        rb = pl.ds(n_chunks - 1 - i, nb, stride=pitch)
        xf_scr[0, rf, :] = x_re
        xf_scr[1, rf, :] = x_im
        xb_scr[0, rb, :] = x_re
        xb_scr[1, rb, :] = x_im
        s_re = jnp.where(is_fwd, s_scr[0, rf, :], s_scr[0, rb, :])
        s_im = jnp.where(is_fwd, s_scr[1, rf, :], s_scr[1, rb, :])
        n_re = a_re * x_re - a_im * x_im + s_re
        n_im = a_re * x_im + a_im * x_re + s_im
        return n_re, n_im

    zero = jnp.zeros((nb, LANES), F32)
    lax.fori_loop(0, n_chunks, step, (zero, zero), unroll=8)
    fwd_rows = lax.broadcasted_iota(jnp.int32, (n_chunks, LANES), 1) < SSM_STATE
    x_in = []
    for b in range(nb):
        rows = slice(b * pitch, b * pitch + n_chunks)
        x_in.append(jnp.concatenate(
            [jnp.where(fwd_rows, xf_scr[part, rows, :], xb_scr[part, rows, :]) for part in range(2)], axis=1))
    x_in = jnp.concatenate(x_in, axis=0).astype(BF16)
    lags = lags_ref[0]
    toep_t = jnp.concatenate(
        [lags[:, (SSM_CHUNK - 1 - t) * SSM_GROUP_CH:(SSM_CHUNK - 1 - t) * SSM_GROUP_CH + CHUNK_COLS]
         for t in range(SSM_CHUNK)], axis=0).astype(BF16)
    y = lax.dot_general(ub, toep_t, (((1,), (1,)), ((), ())), preferred_element_type=F32)
    y = y + jnp.dot(x_in, cm_ref[0], preferred_element_type=F32)
    y_ref[0] = y


def _ssm(u_g, bm, cm, lags, a16_re, a16_im, nb):
    g_n, rows, _ = u_g.shape
    n_chunks = rows // nb
    per_g = lambda g: (g, 0, 0)
    state_scr = pltpu.VMEM((2, nb * (n_chunks + BATCH_PITCH_PAD), LANES), F32)
    return pl.pallas_call(
        functools.partial(_ssm_kernel, nb=nb, n_chunks=n_chunks),
        grid=(g_n,),
        in_specs=[pl.BlockSpec((1, rows, CHUNK_COLS), per_g),
                  pl.BlockSpec((1, CHUNK_COLS, 4 * SSM_STATE), per_g),
                  pl.BlockSpec((1, 4 * SSM_STATE, CHUNK_COLS), per_g),
                  pl.BlockSpec((1, SSM_GROUP_CH, 2 * CHUNK_COLS), per_g),
                  pl.BlockSpec((1, 1, LANES), per_g),
                  pl.BlockSpec((1, 1, LANES), per_g)],
        out_specs=pl.BlockSpec((1, rows, CHUNK_COLS), per_g),
        out_shape=jax.ShapeDtypeStruct((g_n, rows, CHUNK_COLS), F32),
        scratch_shapes=[state_scr, state_scr, state_scr],
        compiler_params=pltpu.CompilerParams(vmem_limit_bytes=VMEM_LIMIT),
        name="ssm_chunked",
    )(u_g, bm, cm, lags, a16_re, a16_im)


def _attn_bias_rows(rpb):
    cols = jnp.arange(GRID_W)
    col_start = jnp.clip(cols - NA_WIN_COLS // 2, 0, GRID_W - NA_WIN_COLS)
    kc = jnp.arange(GRID_W)
    valid = (kc[None, :] >= col_start[:, None]) & (kc[None, :] < col_start[:, None] + NA_WIN_COLS)
    dc = kc[None, :] - cols[:, None] + NA_WIN_COLS - 1
    n_dr, n_dc = 2 * NA_WIN_ROWS - 1, 2 * NA_WIN_COLS - 1
    rpb_ext = jnp.concatenate([rpb.astype(F32), jnp.full(rpb.shape[:2] + (1,), MASK_VALUE, F32)], axis=-1)
    rpb_ext = rpb_ext.reshape(N_HEAD_BLOCKS, HEADS_PER_BLOCK, n_dr, n_dc + 1)
    sel_c = (jnp.where(valid, dc, n_dc)[:, :, None] == jnp.arange(n_dc + 1)).astype(F32)
    by_col = jnp.einsum("bhrd,ckd->brhck", rpb_ext, sel_c, precision=lax.Precision.HIGHEST)
    return by_col.reshape(N_HEAD_BLOCKS, n_dr, HEADS_PER_BLOCK * GRID_W, GRID_W)


def _attn_kernel(q_ref, k_ref, v_ref, bias_ref, o_ref, tab_scr, *, rows_per_step, n_rows):
    wh = min(NA_WIN_ROWS, n_rows)

    @pl.when((pl.program_id(1) == 0) & (pl.program_id(2) == 0))
    def _():
        for case in range(wh):
            for pair in range(wh // 2):
                dr = NA_WIN_ROWS - 1 - case + 2 * pair
                tab_scr[case, :, pair * LANES:(pair + 1) * LANES] = jnp.concatenate(
                    [bias_ref[0, dr], bias_ref[0, dr + 1]], axis=1)

    lane = lax.broadcasted_iota(jnp.int32, (GRID_W, LANES), 1)
    first_head = lane < HEAD_DIM
    zero = jnp.zeros((GRID_W, LANES), BF16)
    kv_rows, scores = [], []
    for j in range(rows_per_step):
        r = pl.program_id(2) * rows_per_step + j
        rs = jnp.clip(r - wh // 2, 0, n_rows - wh)
        q = q_ref[0, pl.ds(pl.multiple_of(r * GRID_W, GRID_W), GRID_W), :]
        qs = jnp.concatenate([jnp.where(first_head, q, zero), jnp.where(first_head, zero, q)], axis=0)
        kv_rows.append(pl.ds(pl.multiple_of(rs * GRID_W, GRID_W), wh * GRID_W))
        kb = k_ref[0, kv_rows[j], :]
        s = lax.dot_general(qs, kb, (((1,), (1,)), ((), ())), preferred_element_type=F32)
        scores.append(s + tab_scr[r - rs])
    probs = [jnp.exp(s - jnp.max(s, axis=-1, keepdims=True)).astype(BF16) for s in scores]
    ones = jnp.ones((wh * GRID_W, LANES), BF16)
    for j in range(rows_per_step):
        v_aug = jnp.concatenate([v_ref[0, kv_rows[j], :], ones], axis=1)
        o = jnp.dot(probs[j], v_aug, preferred_element_type=F32)
        o = o[:, :LANES] / o[:, LANES:]
        o_ref[0, j * GRID_W:(j + 1) * GRID_W, :] = jnp.where(first_head, o[:GRID_W], o[GRID_W:]).astype(BF16)


def _attention(qkv, bias_rows, rows_per_step=32):
    nb, seq, _ = qkv.shape
    n_rows = seq // GRID_W
    wh = min(NA_WIN_ROWS, n_rows)
    assert wh == NA_WIN_ROWS and n_rows % rows_per_step == 0
    nhb = N_HEAD_BLOCKS
    return pl.pallas_call(
        functools.partial(_attn_kernel, rows_per_step=rows_per_step, n_rows=n_rows),
        grid=(nhb, nb, n_rows // rows_per_step),
        in_specs=[pl.BlockSpec((1, seq, LANES), lambda h, b, r: (b, 0, h)),
                  pl.BlockSpec((1, seq, LANES), lambda h, b, r: (b, 0, nhb + h)),
                  pl.BlockSpec((1, seq, LANES), lambda h, b, r: (b, 0, 2 * nhb + h)),
                  pl.BlockSpec((1, 2 * NA_WIN_ROWS - 1, HEADS_PER_BLOCK * GRID_W, GRID_W),
                               lambda h, b, r: (h, 0, 0, 0))],
        out_specs=pl.BlockSpec((1, rows_per_step * GRID_W, LANES), lambda h, b, r: (b, r, h)),
        out_shape=jax.ShapeDtypeStruct((nb, seq, ATTN_WIDTH), BF16),
        scratch_shapes=[pltpu.VMEM((wh, HEADS_PER_BLOCK * GRID_W, wh * GRID_W), F32)],
        compiler_params=pltpu.CompilerParams(vmem_limit_bytes=VMEM_LIMIT),
        name="nbr_attention",
    )(qkv, qkv, qkv, bias_rows)


def _gelu_tanh(x):
    return 0.5 * x * (1.0 + jnp.tanh(math.sqrt(2.0 / math.pi) * (x + 0.044715 * (x * x * x))))


def _mix_ffn_kernel(x_ref, yg_ref, ya_ref, gm_ref, sh_ref, sc_ref, gf_ref, wglu_ref, bglu_ref, ns_ref, na_ref,
                    wout_ref, nf_ref, wg_ref, wu_ref, wd_ref, nfin_ref, o_ref, y_scr, x1_scr):
    n_chunks = yg_ref.shape[1]
    for lb in range(SSM_WIDTH // LANES):
        for th in range(2):
            ys = [yg_ref[lb * PIECES + gl, :, th * LANES:(th + 1) * LANES] for gl in range(PIECES)]
            xs = _piece_transpose(ys)
            for i in range(PIECES):
                y_scr[lb, pl.ds(th * PIECES + i, n_chunks, stride=TOK_PITCH), :] = xs[i]
    sub_chunks = n_chunks // MIX_SUBTILES
    sub = sub_chunks * SSM_CHUNK
    for k in range(MIX_SUBTILES):
        rows = slice(k * sub, (k + 1) * sub)
        y_tok = jnp.concatenate(
            [jnp.concatenate([y_scr[lb, c * TOK_PITCH:c * TOK_PITCH + SSM_CHUNK, :]
                              for c in range(k * sub_chunks, (k + 1) * sub_chunks)], axis=0)
             for lb in range(SSM_WIDTH // LANES)], axis=1)
        ys = _gelu_tanh(y_tok)
        z = jnp.dot(ys.astype(BF16), wglu_ref[...], preferred_element_type=F32) + bglu_ref[...]
        ys = ys * jax.nn.sigmoid(z)
        m_s = _rms(ys, ns_ref[...]).astype(BF16)
        m_a = _rms(ya_ref[0, rows, :].astype(F32), na_ref[...]).astype(BF16)
        o = jnp.dot(m_s, wout_ref[:SSM_WIDTH, :], preferred_element_type=F32)
        o = o + jnp.dot(m_a, wout_ref[SSM_WIDTH:, :], preferred_element_type=F32)
        x1_scr[rows, :] = x_ref[0, rows, :] + gm_ref[0] * o
    for t in range(x1_scr.shape[0] // FFN_TILE):
        rows = slice(t * FFN_TILE, (t + 1) * FFN_TILE)
        x1 = x1_scr[rows, :]
        h = (_rms(x1, nf_ref[...]) * (1.0 + sc_ref[0]) + sh_ref[0]).astype(BF16)
        f = jnp.zeros(x1.shape, F32)
        for j in range(D_FF // FF_CHUNK):
            cols = slice(j * FF_CHUNK, (j + 1) * FF_CHUNK)
            gate = jnp.dot(h, wg_ref[:, cols], preferred_element_type=F32)
            up = jnp.dot(h, wu_ref[:, cols], preferred_element_type=F32)
            act = (gate * jax.nn.sigmoid(gate) * up).astype(BF16)
            f = f + jnp.dot(act, wd_ref[cols, :], preferred_element_type=F32)
        o_ref[0, rows, :] = _rms(x1 + gf_ref[0] * f, nfin_ref[...])


def _mix_ffn(x, y_g, y_att, g_mix, sh_ffn, sc_ffn, g_ffn, p, tm=1024):
    nb, seq, _ = x.shape
    steps = seq // tm
    blk_chunks = tm // SSM_CHUNK
    tok = lambda w: pl.BlockSpec((1, tm, w), lambda b, i: (b, i, 0))
    vec = pl.BlockSpec((1, 1, D_MODEL), lambda b, i: (b, 0, 0))
    const = lambda shape: pl.BlockSpec(shape, lambda b, i: (0,) * len(shape), pipeline_mode=pl.Buffered(1))
    return pl.pallas_call(
        _mix_ffn_kernel,
        grid=(nb, steps),
        in_specs=[tok(D_MODEL),
                  pl.BlockSpec((N_SSM_GROUPS, blk_chunks, CHUNK_COLS), lambda b, i: (0, b * steps + i, 0)),
                  tok(ATTN_WIDTH), vec, vec, vec, vec,
                  const((SSM_WIDTH, SSM_WIDTH)), const((1, SSM_WIDTH)),
                  const((1, SSM_WIDTH)), const((1, ATTN_WIDTH)), const((D_MODEL, D_MODEL)),
                  const((1, D_MODEL)), const((D_MODEL, D_FF)), const((D_MODEL, D_FF)), const((D_FF, D_MODEL)),
                  const((1, D_MODEL))],
        out_specs=tok(D_MODEL),
        out_shape=jax.ShapeDtypeStruct((nb, seq, D_MODEL), F32),
        scratch_shapes=[pltpu.VMEM((SSM_WIDTH // LANES, blk_chunks * TOK_PITCH, LANES), F32),
                        pltpu.VMEM((tm, D_MODEL), F32)],
        compiler_params=pltpu.CompilerParams(vmem_limit_bytes=VMEM_LIMIT),
        name="mix_ffn",
    )(x, y_g, y_att, g_mix, sh_ffn, sc_ffn, g_ffn, p["w_glu"], p["b_glu"], p["norm_ssm_out"], p["norm_attn_out"],
      p["w_out"], p["norm_ffn"], p["w_ffn_gate"], p["w_ffn_up"], p["w_ffn_down"], p["norm_final"])


def _trunk(x, mod, p):
    nb = x.shape[0]
    sh_mix, sc_mix, g_mix, sh_ffn, sc_ffn, g_ffn = [
        mod[:, i * D_MODEL:(i + 1) * D_MODEL].reshape(nb, 1, D_MODEL) for i in range(N_MOD)]
    u_g, qkv = _in_proj(x, sh_mix, sc_mix, p["norm_mix"], p["w_in"])
    y_g = _ssm(u_g, p["bm"], p["cm"], p["lags"], p["a16_re"], p["a16_im"], nb)
    y_att = _attention(qkv, p["bias_rows"])
    return _mix_ffn(x, y_g, y_att, g_mix, sh_ffn, sc_ffn, g_ffn, p)


def kernel(x_prompt, x_sample, c_prompt, c_sample, w_ada, b_ada, norm_mix, w_in, ssm_a_re, ssm_a_im, ssm_log_dt, ssm_b_re, ssm_b_im, ssm_c_re, ssm_c_im, ssm_d, w_glu, b_glu, norm_ssm_out, na_rpb, norm_attn_out, w_out, norm_ffn, w_ffn_gate, w_ffn_up, w_ffn_down, norm_final):
    assert w_ada.shape[0] == 1, "single-layer trunk"
    row = lambda v: v.reshape(1, -1).astype(F32)
    bm, cm, lags, a16_re, a16_im = _ssm_matrices(ssm_a_re[0], ssm_a_im[0], ssm_log_dt[0], ssm_b_re[0],
                                                 ssm_b_im[0], ssm_c_re[0], ssm_c_im[0], ssm_d[0])
    p = dict(
        norm_mix=row(norm_mix[0]), w_in=w_in[0].astype(BF16),
        bm=bm, cm=cm, lags=lags, a16_re=a16_re, a16_im=a16_im,
        bias_rows=_attn_bias_rows(na_rpb[0]),
        w_glu=w_glu[0].astype(BF16), b_glu=row(b_glu[0]),
        norm_ssm_out=row(norm_ssm_out[0]), norm_attn_out=row(norm_attn_out[0]),
        w_out=w_out[0].astype(BF16), norm_ffn=row(norm_ffn[0]),
        w_ffn_gate=w_ffn_gate[0].astype(BF16), w_ffn_up=w_ffn_up[0].astype(BF16),
        w_ffn_down=w_ffn_down[0].astype(BF16), norm_final=row(norm_final),
    )
    n_prompt = x_prompt.shape[0]
    mod = _ada_mod(jnp.concatenate([c_prompt, c_sample], axis=0), w_ada[0], b_ada[0])
    y_prompt = _trunk(x_prompt, mod[:n_prompt], p)
    y_sample = _trunk(x_sample, mod[n_prompt:], p)
    return (y_prompt, y_sample)
```

```python
import functools
import math

import jax
import jax.numpy as jnp
from jax import lax
from jax.experimental import pallas as pl
from jax.experimental.pallas import tpu as pltpu

D_MODEL = 1024
GRID_W = 64
SSM_WIDTH = D_MODEL // 2
SSM_GROUP_CH = 16
N_SSM_GROUPS = SSM_WIDTH // SSM_GROUP_CH
SSM_STATE = 64
ATTN_WIDTH = D_MODEL - SSM_WIDTH
HEAD_DIM = 64
N_HEADS_ATTN = ATTN_WIDTH // HEAD_DIM
IN_PROJ_WIDTH = SSM_WIDTH + 3 * ATTN_WIDTH
NA_WIN_ROWS = 8
NA_WIN_COLS = 16
D_FF = -(-8 * D_MODEL // (3 * 256)) * 256
N_MOD = 6
EPS = 1e-6

LANES = 128
SSM_CHUNK = 16
CHUNK_COLS = SSM_CHUNK * SSM_GROUP_CH
PIECES = LANES // SSM_GROUP_CH
TOK_PITCH = 24
BATCH_PITCH_PAD = 8
HEADS_PER_BLOCK = LANES // HEAD_DIM
N_HEAD_BLOCKS = N_HEADS_ATTN // HEADS_PER_BLOCK
FF_CHUNK = 256
MIX_SUBTILES = 4
FFN_TILE = 512
MASK_VALUE = -1e30
VMEM_LIMIT = 56 * 1024 * 1024

F32 = jnp.float32
BF16 = jnp.bfloat16


def _rms(x, gain):
    return x * lax.rsqrt(jnp.mean(x * x, axis=-1, keepdims=True) + EPS) * gain


def _ada_kernel(c_ref, w_ref, b_ref, o_ref):
    c = c_ref[...]
    s = c * jax.nn.sigmoid(c)
    o_ref[...] = jnp.dot(s, w_ref[...], preferred_element_type=F32,
                         precision=lax.Precision.HIGHEST) + b_ref[...]


def _ada_mod(c, w_ada, b_ada):
    nb = c.shape[0]
    n_out = w_ada.shape[1]
    tn = 1536
    return pl.pallas_call(
        _ada_kernel,
        grid=(n_out // tn,),
        in_specs=[pl.BlockSpec((nb, D_MODEL), lambda j: (0, 0)),
                  pl.BlockSpec((D_MODEL, tn), lambda j: (0, j)),
                  pl.BlockSpec((1, tn), lambda j: (0, j))],
        out_specs=pl.BlockSpec((nb, tn), lambda j: (0, j)),
        out_shape=jax.ShapeDtypeStruct((nb, n_out), F32),
        name="ada_mod",
    )(c, w_ada, b_ada.reshape(1, n_out))


def _piece_transpose(xs):
    assert len(xs) == PIECES
    piece = lax.broadcasted_iota(jnp.int32, xs[0].shape, 1) // SSM_GROUP_CH
    xs = list(xs)
    for d in (4, 2, 1):
        keep = (piece & d) == 0
        nxt = list(xs)
        for a in range(len(xs)):
            if a & d:
                continue
            b = a + d
            nxt[a] = jnp.where(keep, xs[a], pltpu.roll(xs[b], d * SSM_GROUP_CH, axis=1))
            nxt[b] = jnp.where(keep, pltpu.roll(xs[a], LANES - d * SSM_GROUP_CH, axis=1), xs[b])
        xs = nxt
    return xs


def _inproj_kernel(x_ref, sh_ref, sc_ref, g_ref, w_ref, ug_ref, qkv_ref, u_scr):
    x = x_ref[0]
    h = _rms(x, g_ref[...]) * (1.0 + sc_ref[0]) + sh_ref[0]
    hb = h.astype(BF16)
    u = jnp.dot(hb, w_ref[:, :SSM_WIDTH], preferred_element_type=F32)
    n_chunks = u.shape[0] // SSM_CHUNK
    half = PIECES
    for lb in range(SSM_WIDTH // LANES):
        for c in range(n_chunks):
            u_scr[lb, c * TOK_PITCH:c * TOK_PITCH + SSM_CHUNK, :] = (
                u[c * SSM_CHUNK:(c + 1) * SSM_CHUNK, lb * LANES:(lb + 1) * LANES])
    for lb in range(SSM_WIDTH // LANES):
        for th in range(2):
            xs = [u_scr[lb, pl.ds(th * half + i, n_chunks, stride=TOK_PITCH), :] for i in range(half)]
            ys = _piece_transpose(xs)
            for gl in range(half):
                ug_ref[lb * half + gl, :, th * LANES:(th + 1) * LANES] = ys[gl]
    for j in range(3):
        lo = SSM_WIDTH + j * ATTN_WIDTH
        r = jnp.dot(hb, w_ref[:, lo:lo + ATTN_WIDTH], preferred_element_type=F32)
        if j == 0:
            r = r * (HEAD_DIM ** -0.5)
        qkv_ref[0, :, j * ATTN_WIDTH:(j + 1) * ATTN_WIDTH] = r.astype(BF16)


def _in_proj(x, shift, scale, gain, w_in_bf16, tm=1024):
    nb, seq, _ = x.shape
    vec = pl.BlockSpec((1, 1, D_MODEL), lambda b, i: (b, 0, 0))
    steps = seq // tm
    blk_chunks = tm // SSM_CHUNK
    return pl.pallas_call(
        _inproj_kernel,
        grid=(nb, steps),
        in_specs=[pl.BlockSpec((1, tm, D_MODEL), lambda b, i: (b, i, 0)),
                  vec, vec,
                  pl.BlockSpec((1, D_MODEL), lambda b, i: (0, 0)),
                  pl.BlockSpec((D_MODEL, IN_PROJ_WIDTH), lambda b, i: (0, 0))],
        out_specs=[pl.BlockSpec((N_SSM_GROUPS, blk_chunks, CHUNK_COLS), lambda b, i: (0, b * steps + i, 0)),
                   pl.BlockSpec((1, tm, 3 * ATTN_WIDTH), lambda b, i: (b, i, 0))],
        out_shape=[jax.ShapeDtypeStruct((N_SSM_GROUPS, nb * seq // SSM_CHUNK, CHUNK_COLS), F32),
                   jax.ShapeDtypeStruct((nb, seq, 3 * ATTN_WIDTH), BF16)],
        scratch_shapes=[pltpu.VMEM((SSM_WIDTH // LANES, blk_chunks * TOK_PITCH, LANES), F32)],
        compiler_params=pltpu.CompilerParams(vmem_limit_bytes=VMEM_LIMIT),
        name="in_proj",
    )(x, shift, scale, gain, w_in_bf16)


def _cmul(a, b):
    return a[0] * b[0] - a[1] * b[1], a[0] * b[1] + a[1] * b[0]


def _ssm_matrices(a_re, a_im, log_dt, b_re, b_im, c_re, c_im, d_skip):
    hp = lax.Precision.HIGHEST
    c_len, g_n, p_n, h_n = SSM_CHUNK, N_SSM_GROUPS, SSM_STATE, SSM_GROUP_CH
    lam = (jnp.minimum(a_re.astype(F32), -1e-4), a_im.astype(F32))
    dt = jnp.exp(log_dt.astype(F32))[:, :, None]
    z = (lam[0] * dt, lam[1] * dt)
    j = jnp.arange(c_len + 1, dtype=F32)[None, :, None, None]
    mag = jnp.exp(z[0][:, None] * j)
    apow = (mag * jnp.cos(z[1][:, None] * j), mag * jnp.sin(z[1][:, None] * j))
    lam_sq = lam[0] * lam[0] + lam[1] * lam[1]
    zoh = _cmul((apow[0][:, 1] - 1.0, apow[1][:, 1]), (lam[0] / lam_sq, -lam[1] / lam_sq))
    b_bar = _cmul((zoh[0][..., None], zoh[1][..., None]), (b_re.astype(F32), b_im.astype(F32)))
    c = (c_re.astype(F32), c_im.astype(F32))
    c_t = (jnp.swapaxes(c[0], 2, 3), jnp.swapaxes(c[1], 2, 3))
    ap_t = (jnp.transpose(apow[0], (0, 2, 3, 1)), jnp.transpose(apow[1], (0, 2, 3, 1)))
    rep = lambda v: jnp.repeat(v, h_n, axis=-1)
    til = lambda v: jnp.tile(v, (1,) * (v.ndim - 1) + (c_len,))

    w = _cmul((rep(ap_t[0][..., :c_len]), rep(ap_t[1][..., :c_len])), (til(b_bar[0]), til(b_bar[1])))
    kt = jnp.einsum("dghk,dgkn->dghn", jnp.concatenate([c[0], -c[1]], axis=-1),
                    jnp.concatenate(w, axis=2), precision=hp)
    d_gh = d_skip.astype(F32).reshape(g_n, h_n)
    fwd_rev = kt[0].reshape(g_n, h_n, c_len, h_n)[:, :, :0:-1].reshape(g_n, h_n, (c_len - 1) * h_n)
    lag0 = kt[0][..., :h_n] + kt[1][..., :h_n] + jnp.eye(h_n, dtype=F32)[None] * d_gh[:, :, None]
    lags = jnp.concatenate([fwd_rev, lag0, kt[1][..., h_n:], jnp.zeros((g_n, h_n, h_n), F32)], axis=-1)

    bb_t = (jnp.swapaxes(b_bar[0], 2, 3), jnp.swapaxes(b_bar[1], 2, 3))
    as_t = (jnp.transpose(apow[0], (0, 2, 1, 3)), jnp.transpose(apow[1], (0, 2, 1, 3)))
    pw = tuple(jnp.concatenate([v[0, :, c_len - 1::-1], v[1, :, :c_len]], axis=-1)[:, :, None] for v in as_t)
    bb = tuple(jnp.concatenate([v[0], v[1]], axis=-1)[:, None] for v in bb_t)
    bm = jnp.concatenate(_cmul(pw, bb), axis=-1).reshape(g_n, CHUNK_COLS, 4 * p_n)

    c_cat = tuple(jnp.concatenate([v[0], v[1]], axis=1) for v in c_t)
    q_cat = tuple(jnp.concatenate([v[0, :, :, 1:], v[1, :, :, :0:-1]], axis=1) for v in ap_t)
    wq = _cmul((til(c_cat[0]), til(c_cat[1])), (rep(q_cat[0]), rep(q_cat[1])))
    cm = jnp.concatenate([wq[0], -wq[1]], axis=1)

    a16_re = jnp.concatenate([apow[0][0, c_len], apow[0][1, c_len]], axis=-1)[:, None, :]
    a16_im = jnp.concatenate([apow[1][0, c_len], apow[1][1, c_len]], axis=-1)[:, None, :]
    return bm.astype(BF16), cm.astype(BF16), lags, a16_re, a16_im


def _ssm_kernel(*refs, batches, n_chunks):
    n_t = len(batches)
    u_refs, (bm_ref, cm_ref, lags_ref, are_ref, aim_ref) = refs[:n_t], refs[n_t:n_t + 5]
    y_refs, (s_scr, xf_scr, xb_scr) = refs[n_t + 5:2 * n_t + 5], refs[2 * n_t + 5:]
    nb = sum(batches)
    pitch = n_chunks + BATCH_PITCH_PAD
    ub = jnp.concatenate([r[0] for r in u_refs], axis=0).astype(BF16)
    s = jnp.dot(ub, bm_ref[0], preferred_element_type=F32)
    for b in range(nb):
        for part in range(2):
            s_scr[part, b * pitch:b * pitch + n_chunks, :] = (
                s[b * n_chunks:(b + 1) * n_chunks, part * LANES:(part + 1) * LANES])
    a_re = jnp.broadcast_to(are_ref[0], (nb, LANES))
    a_im = jnp.broadcast_to(aim_ref[0], (nb, LANES))
    is_fwd = lax.broadcasted_iota(jnp.int32, (nb, LANES), 1) < SSM_STATE

    def step(i, carry):
        x_re, x_im = carry
        rf = pl.ds(i, nb, stride=pitch)
        rb = pl.ds(n_chunks - 1 - i, nb, stride=pitch)
        xf_scr[0, rf, :] = x_re
        xf_scr[1, rf, :] = x_im
        xb_scr[0, rb, :] = x_re
        xb_scr[1, rb, :] = x_im
        s_re = jnp.where(is_fwd, s_scr[0, rf, :], s_scr[0, rb, :])
        s_im = jnp.where(is_fwd, s_scr[1, rf, :], s_scr[1, rb, :])
        n_re = a_re * x_re - a_im * x_im + s_re
        n_im = a_re * x_im + a_im * x_re + s_im
        return n_re, n_im

    zero = jnp.zeros((nb, LANES), F32)
    lax.fori_loop(0, n_chunks, step, (zero, zero), unroll=8)
    fwd_rows = lax.broadcasted_iota(jnp.int32, (n_chunks, LANES), 1) < SSM_STATE
    x_in = []
    for b in range(nb):
        rows = slice(b * pitch, b * pitch + n_chunks)
        x_in.append(jnp.concatenate(
            [jnp.where(fwd_rows, xf_scr[part, rows, :], xb_scr[part, rows, :]) for part in range(2)], axis=1))
    x_in = jnp.concatenate(x_in, axis=0).astype(BF16)
    lags = lags_ref[0]
    toep_t = jnp.concatenate(
        [lags[:, (SSM_CHUNK - 1 - t) * SSM_GROUP_CH:(SSM_CHUNK - 1 - t) * SSM_GROUP_CH + CHUNK_COLS]
         for t in range(SSM_CHUNK)], axis=0).astype(BF16)
    y = lax.dot_general(ub, toep_t, (((1,), (1,)), ((), ())), preferred_element_type=F32)
    y = y + jnp.dot(x_in, cm_ref[0], preferred_element_type=F32)
    row0 = 0
    for y_ref, b_t in zip(y_refs, batches):
        y_ref[0] = y[row0:row0 + b_t * n_chunks]
        row0 += b_t * n_chunks


def _ssm(u_gs, bm, cm, lags, a16_re, a16_im, batches):
    g_n = u_gs[0].shape[0]
    n_chunks = u_gs[0].shape[1] // batches[0]
    nb = sum(batches)
    per_g = lambda g: (g, 0, 0)
    rows_spec = [pl.BlockSpec((1, u.shape[1], CHUNK_COLS), per_g) for u in u_gs]
    state_scr = pltpu.VMEM((2, nb * (n_chunks + BATCH_PITCH_PAD), LANES), F32)
    return pl.pallas_call(
        functools.partial(_ssm_kernel, batches=tuple(batches), n_chunks=n_chunks),
        grid=(g_n,),
        in_specs=rows_spec + [pl.BlockSpec((1, CHUNK_COLS, 4 * SSM_STATE), per_g),
                              pl.BlockSpec((1, 4 * SSM_STATE, CHUNK_COLS), per_g),
                              pl.BlockSpec((1, SSM_GROUP_CH, 2 * CHUNK_COLS), per_g),
                              pl.BlockSpec((1, 1, LANES), per_g),
                              pl.BlockSpec((1, 1, LANES), per_g)],
        out_specs=rows_spec,
        out_shape=[jax.ShapeDtypeStruct(u.shape, F32) for u in u_gs],
        scratch_shapes=[state_scr, state_scr, state_scr],
        compiler_params=pltpu.CompilerParams(vmem_limit_bytes=VMEM_LIMIT),
        name="ssm_chunked",
    )(*u_gs, bm, cm, lags, a16_re, a16_im)


def _attn_bias_rows(rpb):
    cols = jnp.arange(GRID_W)
    col_start = jnp.clip(cols - NA_WIN_COLS // 2, 0, GRID_W - NA_WIN_COLS)
    kc = jnp.arange(GRID_W)
    valid = (kc[None, :] >= col_start[:, None]) & (kc[None, :] < col_start[:, None] + NA_WIN_COLS)
    dc = kc[None, :] - cols[:, None] + NA_WIN_COLS - 1
    n_dr, n_dc = 2 * NA_WIN_ROWS - 1, 2 * NA_WIN_COLS - 1
    rpb_ext = jnp.concatenate([rpb.astype(F32), jnp.full(rpb.shape[:2] + (1,), MASK_VALUE, F32)], axis=-1)
    rpb_ext = rpb_ext.reshape(N_HEAD_BLOCKS, HEADS_PER_BLOCK, n_dr, n_dc + 1)
    sel_c = (jnp.where(valid, dc, n_dc)[:, :, None] == jnp.arange(n_dc + 1)).astype(F32)
    by_col = jnp.einsum("bhrd,ckd->brhck", rpb_ext, sel_c, precision=lax.Precision.HIGHEST)
    return by_col.reshape(N_HEAD_BLOCKS, n_dr, HEADS_PER_BLOCK * GRID_W, GRID_W)


def _attn_kernel(q_ref, k_ref, v_ref, bias_ref, o_ref, tab_scr, *, rows_per_step, n_rows):
    wh = min(NA_WIN_ROWS, n_rows)

    @pl.when((pl.program_id(1) == 0) & (pl.program_id(2) == 0))
    def _():
        for case in range(wh):
            for pair in range(wh // 2):
                dr = NA_WIN_ROWS - 1 - case + 2 * pair
                tab_scr[case, :, pair * LANES:(pair + 1) * LANES] = jnp.concatenate(
                    [bias_ref[0, dr], bias_ref[0, dr + 1]], axis=1)

    lane = lax.broadcasted_iota(jnp.int32, (GRID_W, LANES), 1)
    first_head = lane < HEAD_DIM
    zero = jnp.zeros((GRID_W, LANES), BF16)
    kv_rows, scores = [], []
    for j in range(rows_per_step):
        r = pl.program_id(2) * rows_per_step + j
        rs = jnp.clip(r - wh // 2, 0, n_rows - wh)
        q = q_ref[0, pl.ds(pl.multiple_of(r * GRID_W, GRID_W), GRID_W), :]
        qs = jnp.concatenate([jnp.where(first_head, q, zero), jnp.where(first_head, zero, q)], axis=0)
        kv_rows.append(pl.ds(pl.multiple_of(rs * GRID_W, GRID_W), wh * GRID_W))
        kb = k_ref[0, kv_rows[j], :]
        s = lax.dot_general(qs, kb, (((1,), (1,)), ((), ())), preferred_element_type=F32)
        scores.append(s + tab_scr[r - rs])
    probs = [jnp.exp(s - jnp.max(s, axis=-1, keepdims=True)).astype(BF16) for s in scores]
    ones = jnp.ones((wh * GRID_W, LANES), BF16)
    for j in range(rows_per_step):
        v_aug = jnp.concatenate([v_ref[0, kv_rows[j], :], ones], axis=1)
        o = jnp.dot(probs[j], v_aug, preferred_element_type=F32)
        o = o[:, :LANES] / o[:, LANES:]
        o_ref[0, j * GRID_W:(j + 1) * GRID_W, :] = jnp.where(first_head, o[:GRID_W], o[GRID_W:]).astype(BF16)


def _attention(qkv, bias_rows, rows_per_step=32):
    nb, seq, _ = qkv.shape
    n_rows = seq // GRID_W
    wh = min(NA_WIN_ROWS, n_rows)
    assert wh == NA_WIN_ROWS and n_rows % rows_per_step == 0
    nhb = N_HEAD_BLOCKS
    return pl.pallas_call(
        functools.partial(_attn_kernel, rows_per_step=rows_per_step, n_rows=n_rows),
        grid=(nhb, nb, n_rows // rows_per_step),
        in_specs=[pl.BlockSpec((1, seq, LANES), lambda h, b, r: (b, 0, h)),
                  pl.BlockSpec((1, seq, LANES), lambda h, b, r: (b, 0, nhb + h)),
                  pl.BlockSpec((1, seq, LANES), lambda h, b, r: (b, 0, 2 * nhb + h)),
                  pl.BlockSpec((1, 2 * NA_WIN_ROWS - 1, HEADS_PER_BLOCK * GRID_W, GRID_W),
                               lambda h, b, r: (h, 0, 0, 0))],
        out_specs=pl.BlockSpec((1, rows_per_step * GRID_W, LANES), lambda h, b, r: (b, r, h)),
        out_shape=jax.ShapeDtypeStruct((nb, seq, ATTN_WIDTH), BF16),
        scratch_shapes=[pltpu.VMEM((wh, HEADS_PER_BLOCK * GRID_W, wh * GRID_W), F32)],
        compiler_params=pltpu.CompilerParams(vmem_limit_bytes=VMEM_LIMIT),
        name="nbr_attention",
    )(qkv, qkv, qkv, bias_rows)


def _gelu_tanh(x):
    return 0.5 * x * (1.0 + jnp.tanh(math.sqrt(2.0 / math.pi) * (x + 0.044715 * (x * x * x))))


def _mix_ffn_kernel(x_ref, yg_ref, ya_ref, gm_ref, sh_ref, sc_ref, gf_ref, wglu_ref, bglu_ref, ns_ref, na_ref,
                    wout_ref, nf_ref, wg_ref, wu_ref, wd_ref, nfin_ref, o_ref, y_scr, x1_scr):
    n_chunks = yg_ref.shape[1]
    for lb in range(SSM_WIDTH // LANES):
        for th in range(2):
            ys = [yg_ref[lb * PIECES + gl, :, th * LANES:(th + 1) * LANES] for gl in range(PIECES)]
            xs = _piece_transpose(ys)
            for i in range(PIECES):
                y_scr[lb, pl.ds(th * PIECES + i, n_chunks, stride=TOK_PITCH), :] = xs[i]
    sub_chunks = n_chunks // MIX_SUBTILES
    sub = sub_chunks * SSM_CHUNK
    for k in range(MIX_SUBTILES):
        rows = slice(k * sub, (k + 1) * sub)
        y_tok = jnp.concatenate(
            [jnp.concatenate([y_scr[lb, c * TOK_PITCH:c * TOK_PITCH + SSM_CHUNK, :]
                              for c in range(k * sub_chunks, (k + 1) * sub_chunks)], axis=0)
             for lb in range(SSM_WIDTH // LANES)], axis=1)
        ys = _gelu_tanh(y_tok)
        z = jnp.dot(ys.astype(BF16), wglu_ref[...], preferred_element_type=F32) + bglu_ref[...]
        ys = ys * jax.nn.sigmoid(z)
        m_s = _rms(ys, ns_ref[...]).astype(BF16)
        m_a = _rms(ya_ref[0, rows, :].astype(F32), na_ref[...]).astype(BF16)
        o = jnp.dot(m_s, wout_ref[:SSM_WIDTH, :], preferred_element_type=F32)
        o = o + jnp.dot(m_a, wout_ref[SSM_WIDTH:, :], preferred_element_type=F32)
        x1_scr[rows, :] = x_ref[0, rows, :] + gm_ref[0] * o
    for t in range(x1_scr.shape[0] // FFN_TILE):
        rows = slice(t * FFN_TILE, (t + 1) * FFN_TILE)
        x1 = x1_scr[rows, :]
        h = (_rms(x1, nf_ref[...]) * (1.0 + sc_ref[0]) + sh_ref[0]).astype(BF16)
        f = jnp.zeros(x1.shape, F32)
        for j in range(D_FF // FF_CHUNK):
            cols = slice(j * FF_CHUNK, (j + 1) * FF_CHUNK)
            gate = jnp.dot(h, wg_ref[:, cols], preferred_element_type=F32)
            up = jnp.dot(h, wu_ref[:, cols], preferred_element_type=F32)
            act = (gate * jax.nn.sigmoid(gate) * up).astype(BF16)
            f = f + jnp.dot(act, wd_ref[cols, :], preferred_element_type=F32)
        o_ref[0, rows, :] = _rms(x1 + gf_ref[0] * f, nfin_ref[...])


def _mix_ffn(x, y_g, y_att, g_mix, sh_ffn, sc_ffn, g_ffn, p, tm=1024):
    nb, seq, _ = x.shape
    steps = seq // tm
    blk_chunks = tm // SSM_CHUNK
    tok = lambda w: pl.BlockSpec((1, tm, w), lambda b, i: (b, i, 0))
    vec = pl.BlockSpec((1, 1, D_MODEL), lambda b, i: (b, 0, 0))
    const = lambda shape: pl.BlockSpec(shape, lambda b, i: (0,) * len(shape), pipeline_mode=pl.Buffered(1))
    return pl.pallas_call(
        _mix_ffn_kernel,
        grid=(nb, steps),
        in_specs=[tok(D_MODEL),
                  pl.BlockSpec((N_SSM_GROUPS, blk_chunks, CHUNK_COLS), lambda b, i: (0, b * steps + i, 0)),
                  tok(ATTN_WIDTH), vec, vec, vec, vec,
                  const((SSM_WIDTH, SSM_WIDTH)), const((1, SSM_WIDTH)),
                  const((1, SSM_WIDTH)), const((1, ATTN_WIDTH)), const((D_MODEL, D_MODEL)),
                  const((1, D_MODEL)), const((D_MODEL, D_FF)), const((D_MODEL, D_FF)), const((D_FF, D_MODEL)),
                  const((1, D_MODEL))],
        out_specs=tok(D_MODEL),
        out_shape=jax.ShapeDtypeStruct((nb, seq, D_MODEL), F32),
        scratch_shapes=[pltpu.VMEM((SSM_WIDTH // LANES, blk_chunks * TOK_PITCH, LANES), F32),
                        pltpu.VMEM((tm, D_MODEL), F32)],
        compiler_params=pltpu.CompilerParams(vmem_limit_bytes=VMEM_LIMIT),
        name="mix_ffn",
    )(x, y_g, y_att, g_mix, sh_ffn, sc_ffn, g_ffn, p["w_glu"], p["b_glu"], p["norm_ssm_out"], p["norm_attn_out"],
      p["w_out"], p["norm_ffn"], p["w_ffn_gate"], p["w_ffn_up"], p["w_ffn_down"], p["norm_final"])


def _trunks(xs, mods, p):
    vecs, u_gs, qkvs = [], [], []
    for x, mod in zip(xs, mods):
        nb = x.shape[0]
        vecs.append([mod[:, i * D_MODEL:(i + 1) * D_MODEL].reshape(nb, 1, D_MODEL) for i in range(N_MOD)])
        sh_mix, sc_mix = vecs[-1][0], vecs[-1][1]
        u_g, qkv = _in_proj(x, sh_mix, sc_mix, p["norm_mix"], p["w_in"])
        u_gs.append(u_g)
        qkvs.append(qkv)
    y_gs = _ssm(u_gs, p["bm"], p["cm"], p["lags"], p["a16_re"], p["a16_im"], [x.shape[0] for x in xs])
    outs = []
    for x, vec, y_g, qkv in zip(xs, vecs, y_gs, qkvs):
        _, _, g_mix, sh_ffn, sc_ffn, g_ffn = vec
        y_att = _attention(qkv, p["bias_rows"])
        outs.append(_mix_ffn(x, y_g, y_att, g_mix, sh_ffn, sc_ffn, g_ffn, p))
    return tuple(outs)


def kernel(x_prompt, x_sample, c_prompt, c_sample, w_ada, b_ada, norm_mix, w_in, ssm_a_re, ssm_a_im, ssm_log_dt, ssm_b_re, ssm_b_im, ssm_c_re, ssm_c_im, ssm_d, w_glu, b_glu, norm_ssm_out, na_rpb, norm_attn_out, w_out, norm_ffn, w_ffn_gate, w_ffn_up, w_ffn_down, norm_final):
    assert w_ada.shape[0] == 1, "single-layer trunk"
    row = lambda v: v.reshape(1, -1).astype(F32)
    bm, cm, lags, a16_re, a16_im = _ssm_matrices(ssm_a_re[0], ssm_a_im[0], ssm_log_dt[0], ssm_b_re[0],
                                                 ssm_b_im[0], ssm_c_re[0], ssm_c_im[0], ssm_d[0])
    p = dict(
        norm_mix=row(norm_mix[0]), w_in=w_in[0].astype(BF16),
        bm=bm, cm=cm, lags=lags, a16_re=a16_re, a16_im=a16_im,
        bias_rows=_attn_bias_rows(na_rpb[0]),
        w_glu=w_glu[0].astype(BF16), b_glu=row(b_glu[0]),
        norm_ssm_out=row(norm_ssm_out[0]), norm_attn_out=row(norm_attn_out[0]),
        w_out=w_out[0].astype(BF16), norm_ffn=row(norm_ffn[0]),
        w_ffn_gate=w_ffn_gate[0].astype(BF16), w_ffn_up=w_ffn_up[0].astype(BF16),
        w_ffn_down=w_ffn_down[0].astype(BF16), norm_final=row(norm_final),
    )
    n_prompt = x_prompt.shape[0]
    mod = _ada_mod(jnp.concatenate([c_prompt, c_sample], axis=0), w_ada[0], b_ada[0])
    return _trunks((x_prompt, x_sample), (mod[:n_prompt], mod[n_prompt:]), p)
```

```python
import functools
import math

import jax
import jax.numpy as jnp
from jax import lax
from jax.experimental import pallas as pl
from jax.experimental.pallas import tpu as pltpu

D_MODEL = 1024
GRID_W = 64
SSM_WIDTH = D_MODEL // 2
SSM_GROUP_CH = 16
N_SSM_GROUPS = SSM_WIDTH // SSM_GROUP_CH
SSM_STATE = 64
ATTN_WIDTH = D_MODEL - SSM_WIDTH
HEAD_DIM = 64
N_HEADS_ATTN = ATTN_WIDTH // HEAD_DIM
IN_PROJ_WIDTH = SSM_WIDTH + 3 * ATTN_WIDTH
NA_WIN_ROWS = 8
NA_WIN_COLS = 16
D_FF = -(-8 * D_MODEL // (3 * 256)) * 256
N_MOD = 6
EPS = 1e-6

LANES = 128
SSM_CHUNK = 16
CHUNK_COLS = SSM_CHUNK * SSM_GROUP_CH
PIECES = LANES // SSM_GROUP_CH
TOK_PITCH = 24
BATCH_PITCH_PAD = 8
HEADS_PER_BLOCK = LANES // HEAD_DIM
N_HEAD_BLOCKS = N_HEADS_ATTN // HEADS_PER_BLOCK
FF_CHUNK = 256
ATTN_GROUP = 2
MIX_SUBTILES = 4
FFN_TILE = 512
MASK_VALUE = -1e30
VMEM_LIMIT = 56 * 1024 * 1024

F32 = jnp.float32
BF16 = jnp.bfloat16


def _rms(x, gain):
    return x * lax.rsqrt(jnp.mean(x * x, axis=-1, keepdims=True) + EPS) * gain


def _ada_kernel(c_ref, w_ref, b_ref, o_ref):
    c = c_ref[...]
    s = c * jax.nn.sigmoid(c)
    o_ref[...] = jnp.dot(s.astype(BF16), w_ref[...].astype(BF16), preferred_element_type=F32) + b_ref[...]


def _ada_mod(c, w_ada, b_ada):
    nb = c.shape[0]
    n_out = w_ada.shape[1]
    tn = 1536
    return pl.pallas_call(
        _ada_kernel,
        grid=(n_out // tn,),
        in_specs=[pl.BlockSpec((nb, D_MODEL), lambda j: (0, 0)),
                  pl.BlockSpec((D_MODEL, tn), lambda j: (0, j)),
                  pl.BlockSpec((1, tn), lambda j: (0, j))],
        out_specs=pl.BlockSpec((nb, tn), lambda j: (0, j)),
        out_shape=jax.ShapeDtypeStruct((nb, n_out), F32),
        name="ada_mod",
    )(c, w_ada, b_ada.reshape(1, n_out))


def _piece_transpose(xs):
    assert len(xs) == PIECES
    piece = lax.broadcasted_iota(jnp.int32, xs[0].shape, 1) // SSM_GROUP_CH
    xs = list(xs)
    for d in (4, 2, 1):
        keep = (piece & d) == 0
        nxt = list(xs)
        for a in range(len(xs)):
            if a & d:
                continue
            b = a + d
            nxt[a] = jnp.where(keep, xs[a], pltpu.roll(xs[b], d * SSM_GROUP_CH, axis=1))
            nxt[b] = jnp.where(keep, pltpu.roll(xs[a], LANES - d * SSM_GROUP_CH, axis=1), xs[b])
        xs = nxt
    return xs


def _inproj_kernel(x_ref, sh_ref, sc_ref, g_ref, w_ref, ug_ref, qkv_ref, u_scr):
    x = x_ref[0]
    h = _rms(x, g_ref[...]) * (1.0 + sc_ref[0]) + sh_ref[0]
    hb = h.astype(BF16)
    u = jnp.dot(hb, w_ref[:, :SSM_WIDTH], preferred_element_type=F32)
    n_chunks = u.shape[0] // SSM_CHUNK
    half = PIECES
    for lb in range(SSM_WIDTH // LANES):
        for c in range(n_chunks):
            u_scr[lb, c * TOK_PITCH:c * TOK_PITCH + SSM_CHUNK, :] = (
                u[c * SSM_CHUNK:(c + 1) * SSM_CHUNK, lb * LANES:(lb + 1) * LANES])
    for lb in range(SSM_WIDTH // LANES):
        for th in range(2):
            xs = [u_scr[lb, pl.ds(th * half + i, n_chunks, stride=TOK_PITCH), :] for i in range(half)]
            ys = _piece_transpose(xs)
            for gl in range(half):
                ug_ref[lb * half + gl, :, th * LANES:(th + 1) * LANES] = ys[gl]
    for j in range(3):
        lo = SSM_WIDTH + j * ATTN_WIDTH
        r = jnp.dot(hb, w_ref[:, lo:lo + ATTN_WIDTH], preferred_element_type=F32)
        if j == 0:
            r = r * (HEAD_DIM ** -0.5)
        qkv_ref[0, :, j * ATTN_WIDTH:(j + 1) * ATTN_WIDTH] = r.astype(BF16)


def _in_proj(x, shift, scale, gain, w_in_bf16, tm=1024):
    nb, seq, _ = x.shape
    vec = pl.BlockSpec((1, 1, D_MODEL), lambda b, i: (b, 0, 0))
    steps = seq // tm
    blk_chunks = tm // SSM_CHUNK
    return pl.pallas_call(
        _inproj_kernel,
        grid=(nb, steps),
        in_specs=[pl.BlockSpec((1, tm, D_MODEL), lambda b, i: (b, i, 0)),
                  vec, vec,
                  pl.BlockSpec((1, D_MODEL), lambda b, i: (0, 0)),
                  pl.BlockSpec((D_MODEL, IN_PROJ_WIDTH), lambda b, i: (0, 0))],
        out_specs=[pl.BlockSpec((N_SSM_GROUPS, blk_chunks, CHUNK_COLS), lambda b, i: (0, b * steps + i, 0)),
                   pl.BlockSpec((1, tm, 3 * ATTN_WIDTH), lambda b, i: (b, i, 0))],
        out_shape=[jax.ShapeDtypeStruct((N_SSM_GROUPS, nb * seq // SSM_CHUNK, CHUNK_COLS), F32),
                   jax.ShapeDtypeStruct((nb, seq, 3 * ATTN_WIDTH), BF16)],
        scratch_shapes=[pltpu.VMEM((SSM_WIDTH // LANES, blk_chunks * TOK_PITCH, LANES), F32)],
        compiler_params=pltpu.CompilerParams(vmem_limit_bytes=VMEM_LIMIT),
        name="in_proj",
    )(x, shift, scale, gain, w_in_bf16)


def _cmul(a, b):
    return a[0] * b[0] - a[1] * b[1], a[0] * b[1] + a[1] * b[0]


def _ssm_matrices(a_re, a_im, log_dt, b_re, b_im, c_re, c_im, d_skip):
    hp = lax.Precision.HIGHEST
    c_len, g_n, p_n, h_n = SSM_CHUNK, N_SSM_GROUPS, SSM_STATE, SSM_GROUP_CH
    lam = (jnp.minimum(a_re.astype(F32), -1e-4), a_im.astype(F32))
    dt = jnp.exp(log_dt.astype(F32))[:, :, None]
    z = (lam[0] * dt, lam[1] * dt)
    j = jnp.arange(c_len + 1, dtype=F32)[None, :, None, None]
    mag = jnp.exp(z[0][:, None] * j)
    apow = (mag * jnp.cos(z[1][:, None] * j), mag * jnp.sin(z[1][:, None] * j))
    lam_sq = lam[0] * lam[0] + lam[1] * lam[1]
    zoh = _cmul((apow[0][:, 1] - 1.0, apow[1][:, 1]), (lam[0] / lam_sq, -lam[1] / lam_sq))
    b_bar = _cmul((zoh[0][..., None], zoh[1][..., None]), (b_re.astype(F32), b_im.astype(F32)))
    c = (c_re.astype(F32), c_im.astype(F32))
    c_t = (jnp.swapaxes(c[0], 2, 3), jnp.swapaxes(c[1], 2, 3))
    ap_t = (jnp.transpose(apow[0], (0, 2, 3, 1)), jnp.transpose(apow[1], (0, 2, 3, 1)))
    rep = lambda v: jnp.repeat(v, h_n, axis=-1)
    til = lambda v: jnp.tile(v, (1,) * (v.ndim - 1) + (c_len,))

    w = _cmul((rep(ap_t[0][..., :c_len]), rep(ap_t[1][..., :c_len])), (til(b_bar[0]), til(b_bar[1])))
    kt = jnp.einsum("dghk,dgkn->dghn", jnp.concatenate([c[0], -c[1]], axis=-1),
                    jnp.concatenate(w, axis=2), precision=hp)
    d_gh = d_skip.astype(F32).reshape(g_n, h_n)
    fwd_rev = kt[0].reshape(g_n, h_n, c_len, h_n)[:, :, :0:-1].reshape(g_n, h_n, (c_len - 1) * h_n)
    lag0 = kt[0][..., :h_n] + kt[1][..., :h_n] + jnp.eye(h_n, dtype=F32)[None] * d_gh[:, :, None]
    lags = jnp.concatenate([fwd_rev, lag0, kt[1][..., h_n:], jnp.zeros((g_n, h_n, h_n), F32)], axis=-1)

    bb_t = (jnp.swapaxes(b_bar[0], 2, 3), jnp.swapaxes(b_bar[1], 2, 3))
    as_t = (jnp.transpose(apow[0], (0, 2, 1, 3)), jnp.transpose(apow[1], (0, 2, 1, 3)))
    pw = tuple(jnp.concatenate([v[0, :, c_len - 1::-1], v[1, :, :c_len]], axis=-1)[:, :, None] for v in as_t)
    bb = tuple(jnp.concatenate([v[0], v[1]], axis=-1)[:, None] for v in bb_t)
    bm = jnp.concatenate(_cmul(pw, bb), axis=-1).reshape(g_n, CHUNK_COLS, 4 * p_n)

    c_cat = tuple(jnp.concatenate([v[0], v[1]], axis=1) for v in c_t)
    q_cat = tuple(jnp.concatenate([v[0, :, :, 1:], v[1, :, :, :0:-1]], axis=1) for v in ap_t)
    wq = _cmul((til(c_cat[0]), til(c_cat[1])), (rep(q_cat[0]), rep(q_cat[1])))
    cm = jnp.concatenate([wq[0], -wq[1]], axis=1)

    a16_re = jnp.concatenate([apow[0][0, c_len], apow[0][1, c_len]], axis=-1)[:, None, :]
    a16_im = jnp.concatenate([apow[1][0, c_len], apow[1][1, c_len]], axis=-1)[:, None, :]
    return bm.astype(BF16), cm.astype(BF16), lags, a16_re, a16_im


def _ssm_kernel(*refs, batches, n_chunks):
    n_t = len(batches)
    u_refs, (bm_ref, cm_ref, lags_ref, are_ref, aim_ref) = refs[:n_t], refs[n_t:n_t + 5]
    y_refs, (s_scr, xf_scr, xb_scr) = refs[n_t + 5:2 * n_t + 5], refs[2 * n_t + 5:]
    nb = sum(batches)
    pitch = n_chunks + BATCH_PITCH_PAD
    ub = jnp.concatenate([r[0] for r in u_refs], axis=0).astype(BF16)
    s = jnp.dot(ub, bm_ref[0], preferred_element_type=F32)
    for b in range(nb):
        for part in range(2):
            s_scr[part, b * pitch:b * pitch + n_chunks, :] = (
                s[b * n_chunks:(b + 1) * n_chunks, part * LANES:(part + 1) * LANES])
    a_re = jnp.broadcast_to(are_ref[0], (nb, LANES))
    a_im = jnp.broadcast_to(aim_ref[0], (nb, LANES))
    is_fwd = lax.broadcasted_iota(jnp.int32, (nb, LANES), 1) < SSM_STATE

    def step(i, carry):
        x_re, x_im = carry
        rf = pl.ds(i, nb, stride=pitch)
        rb = pl.ds(n_chunks - 1 - i, nb, stride=pitch)
        xf_scr[0, rf, :] = x_re
        xf_scr[1, rf, :] = x_im
        xb_scr[0, rb, :] = x_re
        xb_scr[1, rb, :] = x_im
        s_re = jnp.where(is_fwd, s_scr[0, rf, :], s_scr[0, rb, :])
        s_im = jnp.where(is_fwd, s_scr[1, rf, :], s_scr[1, rb, :])
        n_re = a_re * x_re - a_im * x_im + s_re
        n_im = a_re * x_im + a_im * x_re + s_im
        return n_re, n_im

    zero = jnp.zeros((nb, LANES), F32)
    lax.fori_loop(0, n_chunks, step, (zero, zero), unroll=8)
    fwd_rows = lax.broadcasted_iota(jnp.int32, (n_chunks, LANES), 1) < SSM_STATE
    x_in = []
    for b in range(nb):
        rows = slice(b * pitch, b * pitch + n_chunks)
        x_in.append(jnp.concatenate(
            [jnp.where(fwd_rows, xf_scr[part, rows, :], xb_scr[part, rows, :]) for part in range(2)], axis=1))
    x_in = jnp.concatenate(x_in, axis=0).astype(BF16)
    lags = lags_ref[0]
    toep_t = jnp.concatenate(
        [lags[:, (SSM_CHUNK - 1 - t) * SSM_GROUP_CH:(SSM_CHUNK - 1 - t) * SSM_GROUP_CH + CHUNK_COLS]
         for t in range(SSM_CHUNK)], axis=0).astype(BF16)
    y = lax.dot_general(ub, toep_t, (((1,), (1,)), ((), ())), preferred_element_type=F32)
    y = y + jnp.dot(x_in, cm_ref[0], preferred_element_type=F32)
    row0 = 0
    for y_ref, b_t in zip(y_refs, batches):
        y_ref[0] = y[row0:row0 + b_t * n_chunks]
        row0 += b_t * n_chunks


def _ssm(u_gs, bm, cm, lags, a16_re, a16_im, batches):
    g_n = u_gs[0].shape[0]
    n_chunks = u_gs[0].shape[1] // batches[0]
    nb = sum(batches)
    per_g = lambda g: (g, 0, 0)
    rows_spec = [pl.BlockSpec((1, u.shape[1], CHUNK_COLS), per_g) for u in u_gs]
    state_scr = pltpu.VMEM((2, nb * (n_chunks + BATCH_PITCH_PAD), LANES), F32)
    return pl.pallas_call(
        functools.partial(_ssm_kernel, batches=tuple(batches), n_chunks=n_chunks),
        grid=(g_n,),
        in_specs=rows_spec + [pl.BlockSpec((1, CHUNK_COLS, 4 * SSM_STATE), per_g),
                              pl.BlockSpec((1, 4 * SSM_STATE, CHUNK_COLS), per_g),
                              pl.BlockSpec((1, SSM_GROUP_CH, 2 * CHUNK_COLS), per_g),
                              pl.BlockSpec((1, 1, LANES), per_g),
                              pl.BlockSpec((1, 1, LANES), per_g)],
        out_specs=rows_spec,
        out_shape=[jax.ShapeDtypeStruct(u.shape, F32) for u in u_gs],
        scratch_shapes=[state_scr, state_scr, state_scr],
        compiler_params=pltpu.CompilerParams(vmem_limit_bytes=VMEM_LIMIT),
        name="ssm_chunked",
    )(*u_gs, bm, cm, lags, a16_re, a16_im)


def _attn_bias_rows(rpb):
    cols = jnp.arange(GRID_W)
    col_start = jnp.clip(cols - NA_WIN_COLS // 2, 0, GRID_W - NA_WIN_COLS)
    kc = jnp.arange(GRID_W)
    valid = (kc[None, :] >= col_start[:, None]) & (kc[None, :] < col_start[:, None] + NA_WIN_COLS)
    dc = kc[None, :] - cols[:, None] + NA_WIN_COLS - 1
    n_dr, n_dc = 2 * NA_WIN_ROWS - 1, 2 * NA_WIN_COLS - 1
    rpb_ext = jnp.concatenate([rpb.astype(F32), jnp.full(rpb.shape[:2] + (1,), MASK_VALUE, F32)], axis=-1)
    rpb_ext = rpb_ext.reshape(N_HEAD_BLOCKS, HEADS_PER_BLOCK, n_dr, n_dc + 1)
    sel_c = (jnp.where(valid, dc, n_dc)[:, :, None] == jnp.arange(n_dc + 1)).astype(F32)
    by_col = jnp.einsum("bhrd,ckd->brhck", rpb_ext, sel_c, precision=lax.Precision.HIGHEST)
    return by_col.reshape(N_HEAD_BLOCKS, n_dr, HEADS_PER_BLOCK * GRID_W, GRID_W)


def _attn_kernel(q_ref, k_ref, v_ref, bias_ref, o_ref, tab_scr, *, rows_per_step, n_rows):
    wh = min(NA_WIN_ROWS, n_rows)

    @pl.when((pl.program_id(1) == 0) & (pl.program_id(2) == 0))
    def _():
        for case in range(wh):
            for pair in range(wh // 2):
                dr = NA_WIN_ROWS - 1 - case + 2 * pair
                tab_scr[case, :, pair * LANES:(pair + 1) * LANES] = jnp.concatenate(
                    [bias_ref[0, dr], bias_ref[0, dr + 1]], axis=1)

    lane = lax.broadcasted_iota(jnp.int32, (GRID_W, LANES), 1)
    first_head = lane < HEAD_DIM
    zero = jnp.zeros((GRID_W, LANES), BF16)
    ones = jnp.ones((wh * GRID_W, LANES), BF16)
    kv_rows, scores, probs = {}, {}, {}

    def score_phase(j):
        r = pl.program_id(2) * rows_per_step + j
        rs = jnp.clip(r - wh // 2, 0, n_rows - wh)
        q = q_ref[0, pl.ds(pl.multiple_of(r * GRID_W, GRID_W), GRID_W), :]
        qs = jnp.concatenate([jnp.where(first_head, q, zero), jnp.where(first_head, zero, q)], axis=0)
        kv_rows[j] = pl.ds(pl.multiple_of(rs * GRID_W, GRID_W), wh * GRID_W)
        kb = k_ref[0, kv_rows[j], :]
        s = lax.dot_general(qs, kb, (((1,), (1,)), ((), ())), preferred_element_type=F32)
        scores[j] = s + tab_scr[r - rs]

    def softmax_phase(j):
        s = scores.pop(j)
        probs[j] = jnp.exp(s - jnp.max(s, axis=-1, keepdims=True)).astype(BF16)

    def value_phase(j):
        v_aug = jnp.concatenate([v_ref[0, kv_rows[j], :], ones], axis=1)
        o = jnp.dot(probs.pop(j), v_aug, preferred_element_type=F32)
        o = o[:, :LANES] / o[:, LANES:]
        o_ref[0, j * GRID_W:(j + 1) * GRID_W, :] = jnp.where(first_head, o[:GRID_W], o[GRID_W:]).astype(BF16)

    n_groups = rows_per_step // ATTN_GROUP
    for g in range(n_groups + 2):
        for phase, lag in ((score_phase, 0), (softmax_phase, 1), (value_phase, 2)):
            if 0 <= g - lag < n_groups:
                for j in range((g - lag) * ATTN_GROUP, (g - lag + 1) * ATTN_GROUP):
                    phase(j)


def _attention(qkv, bias_rows, rows_per_step=32):
    nb, seq, _ = qkv.shape
    n_rows = seq // GRID_W
    wh = min(NA_WIN_ROWS, n_rows)
    assert wh == NA_WIN_ROWS and n_rows % rows_per_step == 0
    nhb = N_HEAD_BLOCKS
    return pl.pallas_call(
        functools.partial(_attn_kernel, rows_per_step=rows_per_step, n_rows=n_rows),
        grid=(nhb, nb, n_rows // rows_per_step),
        in_specs=[pl.BlockSpec((1, seq, LANES), lambda h, b, r: (b, 0, h)),
                  pl.BlockSpec((1, seq, LANES), lambda h, b, r: (b, 0, nhb + h)),
                  pl.BlockSpec((1, seq, LANES), lambda h, b, r: (b, 0, 2 * nhb + h)),
                  pl.BlockSpec((1, 2 * NA_WIN_ROWS - 1, HEADS_PER_BLOCK * GRID_W, GRID_W),
                               lambda h, b, r: (h, 0, 0, 0))],
        out_specs=pl.BlockSpec((1, rows_per_step * GRID_W, LANES), lambda h, b, r: (b, r, h)),
        out_shape=jax.ShapeDtypeStruct((nb, seq, ATTN_WIDTH), BF16),
        scratch_shapes=[pltpu.VMEM((wh, HEADS_PER_BLOCK * GRID_W, wh * GRID_W), F32)],
        compiler_params=pltpu.CompilerParams(vmem_limit_bytes=VMEM_LIMIT),
        name="nbr_attention",
    )(qkv, qkv, qkv, bias_rows)


def _gelu_tanh(x):
    return 0.5 * x * (1.0 + jnp.tanh(math.sqrt(2.0 / math.pi) * (x + 0.044715 * (x * x * x))))


def _mix_ffn_kernel(x_ref, yg_ref, ya_ref, gm_ref, sh_ref, sc_ref, gf_ref, wglu_ref, bglu_ref, ns_ref, na_ref,
                    wout_ref, nf_ref, wg_ref, wu_ref, wd_ref, nfin_ref, o_ref, y_scr, x1_scr):
    n_chunks = yg_ref.shape[1]
    for lb in range(SSM_WIDTH // LANES):
        for th in range(2):
            ys = [yg_ref[lb * PIECES + gl, :, th * LANES:(th + 1) * LANES] for gl in range(PIECES)]
            xs = _piece_transpose(ys)
            for i in range(PIECES):
                y_scr[lb, pl.ds(th * PIECES + i, n_chunks, stride=TOK_PITCH), :] = xs[i]
    sub_chunks = n_chunks // MIX_SUBTILES
    sub = sub_chunks * SSM_CHUNK
    for k in range(MIX_SUBTILES):
        rows = slice(k * sub, (k + 1) * sub)
        y_tok = jnp.concatenate(
            [jnp.concatenate([y_scr[lb, c * TOK_PITCH:c * TOK_PITCH + SSM_CHUNK, :]
                              for c in range(k * sub_chunks, (k + 1) * sub_chunks)], axis=0)
             for lb in range(SSM_WIDTH // LANES)], axis=1)
        ys = _gelu_tanh(y_tok)
        z = jnp.dot(ys.astype(BF16), wglu_ref[...], preferred_element_type=F32) + bglu_ref[...]
        ys = ys * jax.nn.sigmoid(z)
        m_s = _rms(ys, ns_ref[...]).astype(BF16)
        m_a = _rms(ya_ref[0, rows, :].astype(F32), na_ref[...]).astype(BF16)
        o = jnp.dot(m_s, wout_ref[:SSM_WIDTH, :], preferred_element_type=F32)
        o = o + jnp.dot(m_a, wout_ref[SSM_WIDTH:, :], preferred_element_type=F32)
        x1_scr[rows, :] = x_ref[0, rows, :] + gm_ref[0] * o
    for t in range(x1_scr.shape[0] // FFN_TILE):
        rows = slice(t * FFN_TILE, (t + 1) * FFN_TILE)
        x1 = x1_scr[rows, :]
        h = (_rms(x1, nf_ref[...]) * (1.0 + sc_ref[0]) + sh_ref[0]).astype(BF16)
        f = jnp.zeros(x1.shape, F32)
        for j in range(D_FF // FF_CHUNK):
            cols = slice(j * FF_CHUNK, (j + 1) * FF_CHUNK)
            gate = jnp.dot(h, wg_ref[:, cols], preferred_element_type=F32)
            up = jnp.dot(h, wu_ref[:, cols], preferred_element_type=F32)
            act = (gate * jax.nn.sigmoid(gate) * up).astype(BF16)
            f = f + jnp.dot(act, wd_ref[cols, :], preferred_element_type=F32)
        o_ref[0, rows, :] = _rms(x1 + gf_ref[0] * f, nfin_ref[...])


def _mix_ffn(x, y_g, y_att, g_mix, sh_ffn, sc_ffn, g_ffn, p, tm=1024):
    nb, seq, _ = x.shape
    steps = seq // tm
    blk_chunks = tm // SSM_CHUNK
    tok = lambda w: pl.BlockSpec((1, tm, w), lambda b, i: (b, i, 0))
    vec = pl.BlockSpec((1, 1, D_MODEL), lambda b, i: (b, 0, 0))
    const = lambda shape: pl.BlockSpec(shape, lambda b, i: (0,) * len(shape), pipeline_mode=pl.Buffered(1))
    return pl.pallas_call(
        _mix_ffn_kernel,
        grid=(nb, steps),
        in_specs=[tok(D_MODEL),
                  pl.BlockSpec((N_SSM_GROUPS, blk_chunks, CHUNK_COLS), lambda b, i: (0, b * steps + i, 0)),
                  tok(ATTN_WIDTH), vec, vec, vec, vec,
                  const((SSM_WIDTH, SSM_WIDTH)), const((1, SSM_WIDTH)),
                  const((1, SSM_WIDTH)), const((1, ATTN_WIDTH)), const((D_MODEL, D_MODEL)),
                  const((1, D_MODEL)), const((D_MODEL, D_FF)), const((D_MODEL, D_FF)), const((D_FF, D_MODEL)),
                  const((1, D_MODEL))],
        out_specs=tok(D_MODEL),
        out_shape=jax.ShapeDtypeStruct((nb, seq, D_MODEL), F32),
        scratch_shapes=[pltpu.VMEM((SSM_WIDTH // LANES, blk_chunks * TOK_PITCH, LANES), F32),
                        pltpu.VMEM((tm, D_MODEL), F32)],
        compiler_params=pltpu.CompilerParams(vmem_limit_bytes=VMEM_LIMIT),
        name="mix_ffn",
    )(x, y_g, y_att, g_mix, sh_ffn, sc_ffn, g_ffn, p["w_glu"], p["b_glu"], p["norm_ssm_out"], p["norm_attn_out"],
      p["w_out"], p["norm_ffn"], p["w_ffn_gate"], p["w_ffn_up"], p["w_ffn_down"], p["norm_final"])


def _trunks(xs, mods, p):
    vecs, u_gs, qkvs = [], [], []
    for x, mod in zip(xs, mods):
        nb = x.shape[0]
        vecs.append([mod[:, i * D_MODEL:(i + 1) * D_MODEL].reshape(nb, 1, D_MODEL) for i in range(N_MOD)])
        sh_mix, sc_mix = vecs[-1][0], vecs[-1][1]
        u_g, qkv = _in_proj(x, sh_mix, sc_mix, p["norm_mix"], p["w_in"])
        u_gs.append(u_g)
        qkvs.append(qkv)
    y_gs = _ssm(u_gs, p["bm"], p["cm"], p["lags"], p["a16_re"], p["a16_im"], [x.shape[0] for x in xs])
    outs = []
    for x, vec, y_g, qkv in zip(xs, vecs, y_gs, qkvs):
        _, _, g_mix, sh_ffn, sc_ffn, g_ffn = vec
        y_att = _attention(qkv, p["bias_rows"])
        outs.append(_mix_ffn(x, y_g, y_att, g_mix, sh_ffn, sc_ffn, g_ffn, p))
    return tuple(outs)


def kernel(x_prompt, x_sample, c_prompt, c_sample, w_ada, b_ada, norm_mix, w_in, ssm_a_re, ssm_a_im, ssm_log_dt, ssm_b_re, ssm_b_im, ssm_c_re, ssm_c_im, ssm_d, w_glu, b_glu, norm_ssm_out, na_rpb, norm_attn_out, w_out, norm_ffn, w_ffn_gate, w_ffn_up, w_ffn_down, norm_final):
    assert w_ada.shape[0] == 1, "single-layer trunk"
    row = lambda v: v.reshape(1, -1).astype(F32)
    bm, cm, lags, a16_re, a16_im = _ssm_matrices(ssm_a_re[0], ssm_a_im[0], ssm_log_dt[0], ssm_b_re[0],
                                                 ssm_b_im[0], ssm_c_re[0], ssm_c_im[0], ssm_d[0])
    p = dict(
        norm_mix=row(norm_mix[0]), w_in=w_in[0].astype(BF16),
        bm=bm, cm=cm, lags=lags, a16_re=a16_re, a16_im=a16_im,
        bias_rows=_attn_bias_rows(na_rpb[0]),
        w_glu=w_glu[0].astype(BF16), b_glu=row(b_glu[0]),
        norm_ssm_out=row(norm_ssm_out[0]), norm_attn_out=row(norm_attn_out[0]),
        w_out=w_out[0].astype(BF16), norm_ffn=row(norm_ffn[0]),
        w_ffn_gate=w_ffn_gate[0].astype(BF16), w_ffn_up=w_ffn_up[0].astype(BF16),
        w_ffn_down=w_ffn_down[0].astype(BF16), norm_final=row(norm_final),
    )
    n_prompt = x_prompt.shape[0]
    mod = _ada_mod(jnp.concatenate([c_prompt, c_sample], axis=0), w_ada[0], b_ada[0])
    return _trunks((x_prompt, x_sample), (mod[:n_prompt], mod[n_prompt:]), p)
```

```python
import functools
import math

import jax
import jax.numpy as jnp
from jax import lax
from jax.experimental import pallas as pl
from jax.experimental.pallas import tpu as pltpu

D_MODEL = 1024
GRID_W = 64
SSM_WIDTH = D_MODEL // 2
SSM_GROUP_CH = 16
N_SSM_GROUPS = SSM_WIDTH // SSM_GROUP_CH
SSM_STATE = 64
ATTN_WIDTH = D_MODEL - SSM_WIDTH
HEAD_DIM = 64
N_HEADS_ATTN = ATTN_WIDTH // HEAD_DIM
IN_PROJ_WIDTH = SSM_WIDTH + 3 * ATTN_WIDTH
NA_WIN_ROWS = 8
NA_WIN_COLS = 16
D_FF = -(-8 * D_MODEL // (3 * 256)) * 256
N_MOD = 6
EPS = 1e-6

LANES = 128
SSM_CHUNK = 16
CHUNK_COLS = SSM_CHUNK * SSM_GROUP_CH
PIECES = LANES // SSM_GROUP_CH
TOK_PITCH = 24
BATCH_PITCH_PAD = 8
HEADS_PER_BLOCK = LANES // HEAD_DIM
N_HEAD_BLOCKS = N_HEADS_ATTN // HEADS_PER_BLOCK
FF_CHUNK = 256
ATTN_LAGS = (0, 1, 2)
ATTN_GROUP = 1
MIX_SUBTILES = 4
FFN_TILE = 512
MASK_VALUE = -1e30
VMEM_LIMIT = 56 * 1024 * 1024

F32 = jnp.float32
BF16 = jnp.bfloat16


def _rms(x, gain):
    return x * lax.rsqrt(jnp.mean(x * x, axis=-1, keepdims=True) + EPS) * gain


def _ada_kernel(c_ref, w_ref, b_ref, o_ref):
    c = c_ref[...]
    s = c * jax.nn.sigmoid(c)
    o_ref[...] = jnp.dot(s.astype(BF16), w_ref[...].astype(BF16), preferred_element_type=F32) + b_ref[...]


def _ada_mod(c, w_ada, b_ada):
    nb = c.shape[0]
    n_out = w_ada.shape[1]
    tn = 1536
    return pl.pallas_call(
        _ada_kernel,
        grid=(n_out // tn,),
        in_specs=[pl.BlockSpec((nb, D_MODEL), lambda j: (0, 0)),
                  pl.BlockSpec((D_MODEL, tn), lambda j: (0, j)),
                  pl.BlockSpec((1, tn), lambda j: (0, j))],
        out_specs=pl.BlockSpec((nb, tn), lambda j: (0, j)),
        out_shape=jax.ShapeDtypeStruct((nb, n_out), F32),
        name="ada_mod",
    )(c, w_ada, b_ada.reshape(1, n_out))


def _piece_transpose(xs):
    assert len(xs) == PIECES
    piece = lax.broadcasted_iota(jnp.int32, xs[0].shape, 1) // SSM_GROUP_CH
    xs = list(xs)
    for d in (4, 2, 1):
        keep = (piece & d) == 0
        nxt = list(xs)
        for a in range(len(xs)):
            if a & d:
                continue
            b = a + d
            nxt[a] = jnp.where(keep, xs[a], pltpu.roll(xs[b], d * SSM_GROUP_CH, axis=1))
            nxt[b] = jnp.where(keep, pltpu.roll(xs[a], LANES - d * SSM_GROUP_CH, axis=1), xs[b])
        xs = nxt
    return xs


def _inproj_kernel(x_ref, sh_ref, sc_ref, g_ref, w_ref, ug_ref, qkv_ref, u_scr):
    x = x_ref[0]
    h = _rms(x, g_ref[...]) * (1.0 + sc_ref[0]) + sh_ref[0]
    hb = h.astype(BF16)
    u = jnp.dot(hb, w_ref[:, :SSM_WIDTH], preferred_element_type=F32)
    n_chunks = u.shape[0] // SSM_CHUNK
    half = PIECES
    for lb in range(SSM_WIDTH // LANES):
        for c in range(n_chunks):
            u_scr[lb, c * TOK_PITCH:c * TOK_PITCH + SSM_CHUNK, :] = (
                u[c * SSM_CHUNK:(c + 1) * SSM_CHUNK, lb * LANES:(lb + 1) * LANES])
    for lb in range(SSM_WIDTH // LANES):
        for th in range(2):
            xs = [u_scr[lb, pl.ds(th * half + i, n_chunks, stride=TOK_PITCH), :] for i in range(half)]
            ys = _piece_transpose(xs)
            for gl in range(half):
                ug_ref[lb * half + gl, :, th * LANES:(th + 1) * LANES] = ys[gl]
    for j in range(3):
        lo = SSM_WIDTH + j * ATTN_WIDTH
        r = jnp.dot(hb, w_ref[:, lo:lo + ATTN_WIDTH], preferred_element_type=F32)
        if j == 0:
            r = r * (HEAD_DIM ** -0.5)
        qkv_ref[0, :, j * ATTN_WIDTH:(j + 1) * ATTN_WIDTH] = r.astype(BF16)


def _in_proj(x, shift, scale, gain, w_in_bf16, tm=1024):
    nb, seq, _ = x.shape
    vec = pl.BlockSpec((1, 1, D_MODEL), lambda b, i: (b, 0, 0))
    steps = seq // tm
    blk_chunks = tm // SSM_CHUNK
    return pl.pallas_call(
        _inproj_kernel,
        grid=(nb, steps),
        in_specs=[pl.BlockSpec((1, tm, D_MODEL), lambda b, i: (b, i, 0)),
                  vec, vec,
                  pl.BlockSpec((1, D_MODEL), lambda b, i: (0, 0)),
                  pl.BlockSpec((D_MODEL, IN_PROJ_WIDTH), lambda b, i: (0, 0))],
        out_specs=[pl.BlockSpec((N_SSM_GROUPS, blk_chunks, CHUNK_COLS), lambda b, i: (0, b * steps + i, 0)),
                   pl.BlockSpec((1, tm, 3 * ATTN_WIDTH), lambda b, i: (b, i, 0))],
        out_shape=[jax.ShapeDtypeStruct((N_SSM_GROUPS, nb * seq // SSM_CHUNK, CHUNK_COLS), F32),
                   jax.ShapeDtypeStruct((nb, seq, 3 * ATTN_WIDTH), BF16)],
        scratch_shapes=[pltpu.VMEM((SSM_WIDTH // LANES, blk_chunks * TOK_PITCH, LANES), F32)],
        compiler_params=pltpu.CompilerParams(vmem_limit_bytes=VMEM_LIMIT),
        name="in_proj",
    )(x, shift, scale, gain, w_in_bf16)


def _cmul(a, b):
    return a[0] * b[0] - a[1] * b[1], a[0] * b[1] + a[1] * b[0]


def _ssm_matrices(a_re, a_im, log_dt, b_re, b_im, c_re, c_im, d_skip):
    hp = lax.Precision.HIGHEST
    c_len, g_n, p_n, h_n = SSM_CHUNK, N_SSM_GROUPS, SSM_STATE, SSM_GROUP_CH
    lam = (jnp.minimum(a_re.astype(F32), -1e-4), a_im.astype(F32))
    dt = jnp.exp(log_dt.astype(F32))[:, :, None]
    z = (lam[0] * dt, lam[1] * dt)
    j = jnp.arange(c_len + 1, dtype=F32)[None, :, None, None]
    mag = jnp.exp(z[0][:, None] * j)
    apow = (mag * jnp.cos(z[1][:, None] * j), mag * jnp.sin(z[1][:, None] * j))
    lam_sq = lam[0] * lam[0] + lam[1] * lam[1]
    zoh = _cmul((apow[0][:, 1] - 1.0, apow[1][:, 1]), (lam[0] / lam_sq, -lam[1] / lam_sq))
    b_bar = _cmul((zoh[0][..., None], zoh[1][..., None]), (b_re.astype(F32), b_im.astype(F32)))
    c = (c_re.astype(F32), c_im.astype(F32))
    c_t = (jnp.swapaxes(c[0], 2, 3), jnp.swapaxes(c[1], 2, 3))
    ap_t = (jnp.transpose(apow[0], (0, 2, 3, 1)), jnp.transpose(apow[1], (0, 2, 3, 1)))
    rep = lambda v: jnp.repeat(v, h_n, axis=-1)
    til = lambda v: jnp.tile(v, (1,) * (v.ndim - 1) + (c_len,))

    w = _cmul((rep(ap_t[0][..., :c_len]), rep(ap_t[1][..., :c_len])), (til(b_bar[0]), til(b_bar[1])))
    kt = jnp.einsum("dghk,dgkn->dghn", jnp.concatenate([c[0], -c[1]], axis=-1),
                    jnp.concatenate(w, axis=2), precision=hp)
    d_gh = d_skip.astype(F32).reshape(g_n, h_n)
    fwd_rev = kt[0].reshape(g_n, h_n, c_len, h_n)[:, :, :0:-1].reshape(g_n, h_n, (c_len - 1) * h_n)
    lag0 = kt[0][..., :h_n] + kt[1][..., :h_n] + jnp.eye(h_n, dtype=F32)[None] * d_gh[:, :, None]
    lags = jnp.concatenate([fwd_rev, lag0, kt[1][..., h_n:], jnp.zeros((g_n, h_n, h_n), F32)], axis=-1)

    bb_t = (jnp.swapaxes(b_bar[0], 2, 3), jnp.swapaxes(b_bar[1], 2, 3))
    as_t = (jnp.transpose(apow[0], (0, 2, 1, 3)), jnp.transpose(apow[1], (0, 2, 1, 3)))
    pw = tuple(jnp.concatenate([v[0, :, c_len - 1::-1], v[1, :, :c_len]], axis=-1)[:, :, None] for v in as_t)
    bb = tuple(jnp.concatenate([v[0], v[1]], axis=-1)[:, None] for v in bb_t)
    bm = jnp.concatenate(_cmul(pw, bb), axis=-1).reshape(g_n, CHUNK_COLS, 4 * p_n)

    c_cat = tuple(jnp.concatenate([v[0], v[1]], axis=1) for v in c_t)
    q_cat = tuple(jnp.concatenate([v[0, :, :, 1:], v[1, :, :, :0:-1]], axis=1) for v in ap_t)
    wq = _cmul((til(c_cat[0]), til(c_cat[1])), (rep(q_cat[0]), rep(q_cat[1])))
    cm = jnp.concatenate([wq[0], -wq[1]], axis=1)

    a16_re = jnp.concatenate([apow[0][0, c_len], apow[0][1, c_len]], axis=-1)[:, None, :]
    a16_im = jnp.concatenate([apow[1][0, c_len], apow[1][1, c_len]], axis=-1)[:, None, :]
    return bm.astype(BF16), cm.astype(BF16), lags, a16_re, a16_im


def _ssm_kernel(*refs, batches, n_chunks):
    n_t = len(batches)
    u_refs, (bm_ref, cm_ref, lags_ref, are_ref, aim_ref) = refs[:n_t], refs[n_t:n_t + 5]
    y_refs, (s_scr, xf_scr, xb_scr) = refs[n_t + 5:2 * n_t + 5], refs[2 * n_t + 5:]
    nb = sum(batches)
    pitch = n_chunks + BATCH_PITCH_PAD
    ub = jnp.concatenate([r[0] for r in u_refs], axis=0).astype(BF16)
    s = jnp.dot(ub, bm_ref[0], preferred_element_type=F32)
    for b in range(nb):
        for part in range(2):
            s_scr[part, b * pitch:b * pitch + n_chunks, :] = (
                s[b * n_chunks:(b + 1) * n_chunks, part * LANES:(part + 1) * LANES])
    a_re = jnp.broadcast_to(are_ref[0], (nb, LANES))
    a_im = jnp.broadcast_to(aim_ref[0], (nb, LANES))
    is_fwd = lax.broadcasted_iota(jnp.int32, (nb, LANES), 1) < SSM_STATE

    def step(i, carry):
        x_re, x_im = carry
        rf = pl.ds(i, nb, stride=pitch)
        rb = pl.ds(n_chunks - 1 - i, nb, stride=pitch)
        xf_scr[0, rf, :] = x_re
        xf_scr[1, rf, :] = x_im
        xb_scr[0, rb, :] = x_re
        xb_scr[1, rb, :] = x_im
        s_re = jnp.where(is_fwd, s_scr[0, rf, :], s_scr[0, rb, :])
        s_im = jnp.where(is_fwd, s_scr[1, rf, :], s_scr[1, rb, :])
        n_re = a_re * x_re - a_im * x_im + s_re
        n_im = a_re * x_im + a_im * x_re + s_im
        return n_re, n_im

    zero = jnp.zeros((nb, LANES), F32)
    lax.fori_loop(0, n_chunks, step, (zero, zero), unroll=8)
    fwd_rows = lax.broadcasted_iota(jnp.int32, (n_chunks, LANES), 1) < SSM_STATE
    x_in = []
    for b in range(nb):
        rows = slice(b * pitch, b * pitch + n_chunks)
        x_in.append(jnp.concatenate(
            [jnp.where(fwd_rows, xf_scr[part, rows, :], xb_scr[part, rows, :]) for part in range(2)], axis=1))
    x_in = jnp.concatenate(x_in, axis=0).astype(BF16)
    lags = lags_ref[0]
    toep_t = jnp.concatenate(
        [lags[:, (SSM_CHUNK - 1 - t) * SSM_GROUP_CH:(SSM_CHUNK - 1 - t) * SSM_GROUP_CH + CHUNK_COLS]
         for t in range(SSM_CHUNK)], axis=0).astype(BF16)
    y = lax.dot_general(ub, toep_t, (((1,), (1,)), ((), ())), preferred_element_type=F32)
    y = y + jnp.dot(x_in, cm_ref[0], preferred_element_type=F32)
    row0 = 0
    for y_ref, b_t in zip(y_refs, batches):
        y_ref[0] = y[row0:row0 + b_t * n_chunks]
        row0 += b_t * n_chunks


def _ssm(u_gs, bm, cm, lags, a16_re, a16_im, batches):
    g_n = u_gs[0].shape[0]
    n_chunks = u_gs[0].shape[1] // batches[0]
    nb = sum(batches)
    per_g = lambda g: (g, 0, 0)
    rows_spec = [pl.BlockSpec((1, u.shape[1], CHUNK_COLS), per_g) for u in u_gs]
    state_scr = pltpu.VMEM((2, nb * (n_chunks + BATCH_PITCH_PAD), LANES), F32)
    return pl.pallas_call(
        functools.partial(_ssm_kernel, batches=tuple(batches), n_chunks=n_chunks),
        grid=(g_n,),
        in_specs=rows_spec + [pl.BlockSpec((1, CHUNK_COLS, 4 * SSM_STATE), per_g),
                              pl.BlockSpec((1, 4 * SSM_STATE, CHUNK_COLS), per_g),
                              pl.BlockSpec((1, SSM_GROUP_CH, 2 * CHUNK_COLS), per_g),
                              pl.BlockSpec((1, 1, LANES), per_g),
                              pl.BlockSpec((1, 1, LANES), per_g)],
        out_specs=rows_spec,
        out_shape=[jax.ShapeDtypeStruct(u.shape, F32) for u in u_gs],
        scratch_shapes=[state_scr, state_scr, state_scr],
        compiler_params=pltpu.CompilerParams(vmem_limit_bytes=VMEM_LIMIT),
        name="ssm_chunked",
    )(*u_gs, bm, cm, lags, a16_re, a16_im)


def _attn_bias_rows(rpb):
    cols = jnp.arange(GRID_W)
    col_start = jnp.clip(cols - NA_WIN_COLS // 2, 0, GRID_W - NA_WIN_COLS)
    kc = jnp.arange(GRID_W)
    valid = (kc[None, :] >= col_start[:, None]) & (kc[None, :] < col_start[:, None] + NA_WIN_COLS)
    dc = kc[None, :] - cols[:, None] + NA_WIN_COLS - 1
    n_dr, n_dc = 2 * NA_WIN_ROWS - 1, 2 * NA_WIN_COLS - 1
    rpb_ext = jnp.concatenate([rpb.astype(F32), jnp.full(rpb.shape[:2] + (1,), MASK_VALUE, F32)], axis=-1)
    rpb_ext = rpb_ext.reshape(N_HEAD_BLOCKS, HEADS_PER_BLOCK, n_dr, n_dc + 1)
    sel_c = (jnp.where(valid, dc, n_dc)[:, :, None] == jnp.arange(n_dc + 1)).astype(F32)
    by_col = jnp.einsum("bhrd,ckd->brhck", rpb_ext, sel_c, precision=lax.Precision.HIGHEST)
    return by_col.reshape(N_HEAD_BLOCKS, n_dr, HEADS_PER_BLOCK * GRID_W, GRID_W)


def _attn_kernel(q_ref, k_ref, v_ref, bias_ref, o_ref, tab_scr, *, rows_per_step, n_rows):
    wh = min(NA_WIN_ROWS, n_rows)

    @pl.when((pl.program_id(1) == 0) & (pl.program_id(2) == 0))
    def _():
        for case in range(wh):
            for pair in range(wh // 2):
                dr = NA_WIN_ROWS - 1 - case + 2 * pair
                tab_scr[case, :, pair * LANES:(pair + 1) * LANES] = jnp.concatenate(
                    [bias_ref[0, dr], bias_ref[0, dr + 1]], axis=1)

    lane = lax.broadcasted_iota(jnp.int32, (GRID_W, LANES), 1)
    first_head = lane < HEAD_DIM
    zero = jnp.zeros((GRID_W, LANES), BF16)
    ones = jnp.ones((wh * GRID_W, LANES), BF16)
    kv_rows, scores, probs = {}, {}, {}

    def score_phase(j):
        r = pl.program_id(2) * rows_per_step + j
        rs = jnp.clip(r - wh // 2, 0, n_rows - wh)
        q = q_ref[0, pl.ds(pl.multiple_of(r * GRID_W, GRID_W), GRID_W), :]
        qs = jnp.concatenate([jnp.where(first_head, q, zero), jnp.where(first_head, zero, q)], axis=0)
        kv_rows[j] = pl.ds(pl.multiple_of(rs * GRID_W, GRID_W), wh * GRID_W)
        kb = k_ref[0, kv_rows[j], :]
        s = lax.dot_general(qs, kb, (((1,), (1,)), ((), ())), preferred_element_type=F32)
        scores[j] = s + tab_scr[r - rs]

    def softmax_phase(j):
        s = scores.pop(j)
        probs[j] = jnp.exp(s - jnp.max(s, axis=-1, keepdims=True)).astype(BF16)

    def value_phase(j):
        v_aug = jnp.concatenate([v_ref[0, kv_rows[j], :], ones], axis=1)
        o = jnp.dot(probs.pop(j), v_aug, preferred_element_type=F32)
        o = o[:, :LANES] / o[:, LANES:]
        o_ref[0, j * GRID_W:(j + 1) * GRID_W, :] = jnp.where(first_head, o[:GRID_W], o[GRID_W:]).astype(BF16)

    n_groups = rows_per_step // ATTN_GROUP
    for g in range(n_groups + ATTN_LAGS[-1]):
        for phase, lag in zip((score_phase, softmax_phase, value_phase), ATTN_LAGS):
            if 0 <= g - lag < n_groups:
                for j in range((g - lag) * ATTN_GROUP, (g - lag + 1) * ATTN_GROUP):
                    phase(j)


def _attention(qkv, bias_rows, rows_per_step=32):
    nb, seq, _ = qkv.shape
    n_rows = seq // GRID_W
    wh = min(NA_WIN_ROWS, n_rows)
    assert wh == NA_WIN_ROWS and n_rows % rows_per_step == 0
    nhb = N_HEAD_BLOCKS
    return pl.pallas_call(
        functools.partial(_attn_kernel, rows_per_step=rows_per_step, n_rows=n_rows),
        grid=(nhb, nb, n_rows // rows_per_step),
        in_specs=[pl.BlockSpec((1, seq, LANES), lambda h, b, r: (b, 0, h)),
                  pl.BlockSpec((1, seq, LANES), lambda h, b, r: (b, 0, nhb + h)),
                  pl.BlockSpec((1, seq, LANES), lambda h, b, r: (b, 0, 2 * nhb + h)),
                  pl.BlockSpec((1, 2 * NA_WIN_ROWS - 1, HEADS_PER_BLOCK * GRID_W, GRID_W),
                               lambda h, b, r: (h, 0, 0, 0))],
        out_specs=pl.BlockSpec((1, rows_per_step * GRID_W, LANES), lambda h, b, r: (b, r, h)),
        out_shape=jax.ShapeDtypeStruct((nb, seq, ATTN_WIDTH), BF16),
        scratch_shapes=[pltpu.VMEM((wh, HEADS_PER_BLOCK * GRID_W, wh * GRID_W), F32)],
        compiler_params=pltpu.CompilerParams(vmem_limit_bytes=VMEM_LIMIT),
        name="nbr_attention",
    )(qkv, qkv, qkv, bias_rows)


def _gelu_tanh(x):
    return 0.5 * x * (1.0 + jnp.tanh(math.sqrt(2.0 / math.pi) * (x + 0.044715 * (x * x * x))))


def _mix_ffn_kernel(x_ref, yg_ref, ya_ref, gm_ref, sh_ref, sc_ref, gf_ref, wglu_ref, bglu_ref, ns_ref, na_ref,
                    wout_ref, nf_ref, wg_ref, wu_ref, wd_ref, nfin_ref, o_ref, y_scr, x1_scr):
    n_chunks = yg_ref.shape[1]
    for lb in range(SSM_WIDTH // LANES):
        for th in range(2):
            ys = [yg_ref[lb * PIECES + gl, :, th * LANES:(th + 1) * LANES] for gl in range(PIECES)]
            xs = _piece_transpose(ys)
            for i in range(PIECES):
                y_scr[lb, pl.ds(th * PIECES + i, n_chunks, stride=TOK_PITCH), :] = xs[i]
    sub_chunks = n_chunks // MIX_SUBTILES
    sub = sub_chunks * SSM_CHUNK
    for k in range(MIX_SUBTILES):
        rows = slice(k * sub, (k + 1) * sub)
        y_tok = jnp.concatenate(
            [jnp.concatenate([y_scr[lb, c * TOK_PITCH:c * TOK_PITCH + SSM_CHUNK, :]
                              for c in range(k * sub_chunks, (k + 1) * sub_chunks)], axis=0)
             for lb in range(SSM_WIDTH // LANES)], axis=1)
        ys = _gelu_tanh(y_tok)
        z = jnp.dot(ys.astype(BF16), wglu_ref[...], preferred_element_type=F32) + bglu_ref[...]
        ys = ys * jax.nn.sigmoid(z)
        m_s = _rms(ys, ns_ref[...]).astype(BF16)
        m_a = _rms(ya_ref[0, rows, :].astype(F32), na_ref[...]).astype(BF16)
        o = jnp.dot(m_s, wout_ref[:SSM_WIDTH, :], preferred_element_type=F32)
        o = o + jnp.dot(m_a, wout_ref[SSM_WIDTH:, :], preferred_element_type=F32)
        x1_scr[rows, :] = x_ref[0, rows, :] + gm_ref[0] * o
    for t in range(x1_scr.shape[0] // FFN_TILE):
        rows = slice(t * FFN_TILE, (t + 1) * FFN_TILE)
        x1 = x1_scr[rows, :]
        h = (_rms(x1, nf_ref[...]) * (1.0 + sc_ref[0]) + sh_ref[0]).astype(BF16)
        f = jnp.zeros(x1.shape, F32)
        for j in range(D_FF // FF_CHUNK):
            cols = slice(j * FF_CHUNK, (j + 1) * FF_CHUNK)
            gate = jnp.dot(h, wg_ref[:, cols], preferred_element_type=F32)
            up = jnp.dot(h, wu_ref[:, cols], preferred_element_type=F32)
            act = (gate * jax.nn.sigmoid(gate) * up).astype(BF16)
            f = f + jnp.dot(act, wd_ref[cols, :], preferred_element_type=F32)
        o_ref[0, rows, :] = _rms(x1 + gf_ref[0] * f, nfin_ref[...])


def _mix_ffn(x, y_g, y_att, g_mix, sh_ffn, sc_ffn, g_ffn, p, tm=1024):
    nb, seq, _ = x.shape
    steps = seq // tm
    blk_chunks = tm // SSM_CHUNK
    tok = lambda w: pl.BlockSpec((1, tm, w), lambda b, i: (b, i, 0))
    vec = pl.BlockSpec((1, 1, D_MODEL), lambda b, i: (b, 0, 0))
    const = lambda shape: pl.BlockSpec(shape, lambda b, i: (0,) * len(shape), pipeline_mode=pl.Buffered(1))
    return pl.pallas_call(
        _mix_ffn_kernel,
        grid=(nb, steps),
        in_specs=[tok(D_MODEL),
                  pl.BlockSpec((N_SSM_GROUPS, blk_chunks, CHUNK_COLS), lambda b, i: (0, b * steps + i, 0)),
                  tok(ATTN_WIDTH), vec, vec, vec, vec,
                  const((SSM_WIDTH, SSM_WIDTH)), const((1, SSM_WIDTH)),
                  const((1, SSM_WIDTH)), const((1, ATTN_WIDTH)), const((D_MODEL, D_MODEL)),
                  const((1, D_MODEL)), const((D_MODEL, D_FF)), const((D_MODEL, D_FF)), const((D_FF, D_MODEL)),
                  const((1, D_MODEL))],
        out_specs=tok(D_MODEL),
        out_shape=jax.ShapeDtypeStruct((nb, seq, D_MODEL), F32),
        scratch_shapes=[pltpu.VMEM((SSM_WIDTH // LANES, blk_chunks * TOK_PITCH, LANES), F32),
                        pltpu.VMEM((tm, D_MODEL), F32)],
        compiler_params=pltpu.CompilerParams(vmem_limit_bytes=VMEM_LIMIT),
        name="mix_ffn",
    )(x, y_g, y_att, g_mix, sh_ffn, sc_ffn, g_ffn, p["w_glu"], p["b_glu"], p["norm_ssm_out"], p["norm_attn_out"],
      p["w_out"], p["norm_ffn"], p["w_ffn_gate"], p["w_ffn_up"], p["w_ffn_down"], p["norm_final"])


def _trunks(xs, mods, p):
    vecs, u_gs, qkvs = [], [], []
    for x, mod in zip(xs, mods):
        nb = x.shape[0]
        vecs.append([mod[:, i * D_MODEL:(i + 1) * D_MODEL].reshape(nb, 1, D_MODEL) for i in range(N_MOD)])
        sh_mix, sc_mix = vecs[-1][0], vecs[-1][1]
        u_g, qkv = _in_proj(x, sh_mix, sc_mix, p["norm_mix"], p["w_in"])
        u_gs.append(u_g)
        qkvs.append(qkv)
    y_gs = _ssm(u_gs, p["bm"], p["cm"], p["lags"], p["a16_re"], p["a16_im"], [x.shape[0] for x in xs])
    outs = []
    for x, vec, y_g, qkv in zip(xs, vecs, y_gs, qkvs):
        _, _, g_mix, sh_ffn, sc_ffn, g_ffn = vec
        y_att = _attention(qkv, p["bias_rows"])
        outs.append(_mix_ffn(x, y_g, y_att, g_mix, sh_ffn, sc_ffn, g_ffn, p))
    return tuple(outs)


def kernel(x_prompt, x_sample, c_prompt, c_sample, w_ada, b_ada, norm_mix, w_in, ssm_a_re, ssm_a_im, ssm_log_dt, ssm_b_re, ssm_b_im, ssm_c_re, ssm_c_im, ssm_d, w_glu, b_glu, norm_ssm_out, na_rpb, norm_attn_out, w_out, norm_ffn, w_ffn_gate, w_ffn_up, w_ffn_down, norm_final):
    assert w_ada.shape[0] == 1, "single-layer trunk"
    row = lambda v: v.reshape(1, -1).astype(F32)
    bm, cm, lags, a16_re, a16_im = _ssm_matrices(ssm_a_re[0], ssm_a_im[0], ssm_log_dt[0], ssm_b_re[0],
                                                 ssm_b_im[0], ssm_c_re[0], ssm_c_im[0], ssm_d[0])
    p = dict(
        norm_mix=row(norm_mix[0]), w_in=w_in[0].astype(BF16),
        bm=bm, cm=cm, lags=lags, a16_re=a16_re, a16_im=a16_im,
        bias_rows=_attn_bias_rows(na_rpb[0]),
        w_glu=w_glu[0].astype(BF16), b_glu=row(b_glu[0]),
        norm_ssm_out=row(norm_ssm_out[0]), norm_attn_out=row(norm_attn_out[0]),
        w_out=w_out[0].astype(BF16), norm_ffn=row(norm_ffn[0]),
        w_ffn_gate=w_ffn_gate[0].astype(BF16), w_ffn_up=w_ffn_up[0].astype(BF16),
        w_ffn_down=w_ffn_down[0].astype(BF16), norm_final=row(norm_final),
    )
    n_prompt = x_prompt.shape[0]
    mod = _ada_mod(jnp.concatenate([c_prompt, c_sample], axis=0), w_ada[0], b_ada[0])
    return _trunks((x_prompt, x_sample), (mod[:n_prompt], mod[n_prompt:]), p)
```

```python
import functools
import math

import jax
import jax.numpy as jnp
from jax import lax
from jax.experimental import pallas as pl
from jax.experimental.pallas import tpu as pltpu

D_MODEL = 1024
GRID_W = 64
SSM_WIDTH = D_MODEL // 2
SSM_GROUP_CH = 16
N_SSM_GROUPS = SSM_WIDTH // SSM_GROUP_CH
SSM_STATE = 64
ATTN_WIDTH = D_MODEL - SSM_WIDTH
HEAD_DIM = 64
N_HEADS_ATTN = ATTN_WIDTH // HEAD_DIM
IN_PROJ_WIDTH = SSM_WIDTH + 3 * ATTN_WIDTH
NA_WIN_ROWS = 8
NA_WIN_COLS = 16
D_FF = -(-8 * D_MODEL // (3 * 256)) * 256
N_MOD = 6
EPS = 1e-6

LANES = 128
SSM_CHUNK = 16
CHUNK_COLS = SSM_CHUNK * SSM_GROUP_CH
PIECES = LANES // SSM_GROUP_CH
TOK_PITCH = 24
BATCH_PITCH_PAD = 8
HEADS_PER_BLOCK = LANES // HEAD_DIM
N_HEAD_BLOCKS = N_HEADS_ATTN // HEADS_PER_BLOCK
FF_CHUNK = 256
ATTN_LAGS = (0, 1, 2)
ATTN_GROUP = 1
MIX_SUBTILES = 4
FFN_TILE = 512
MASK_VALUE = -1e30
VMEM_LIMIT = 56 * 1024 * 1024

F32 = jnp.float32
BF16 = jnp.bfloat16


def _rms(x, gain):
    return x * lax.rsqrt(jnp.mean(x * x, axis=-1, keepdims=True) + EPS) * gain


def _ada_kernel(c_ref, w_ref, b_ref, o_ref):
    c = c_ref[...]
    s = c * jax.nn.sigmoid(c)
    o_ref[...] = jnp.dot(s.astype(BF16), w_ref[...].astype(BF16), preferred_element_type=F32) + b_ref[...]


def _ada_mod(c, w_ada, b_ada):
    nb = c.shape[0]
    n_out = w_ada.shape[1]
    tn = 1536
    return pl.pallas_call(
        _ada_kernel,
        grid=(n_out // tn,),
        in_specs=[pl.BlockSpec((nb, D_MODEL), lambda j: (0, 0)),
                  pl.BlockSpec((D_MODEL, tn), lambda j: (0, j)),
                  pl.BlockSpec((1, tn), lambda j: (0, j))],
        out_specs=pl.BlockSpec((nb, tn), lambda j: (0, j)),
        out_shape=jax.ShapeDtypeStruct((nb, n_out), F32),
        name="ada_mod",
    )(c, w_ada, b_ada.reshape(1, n_out))


def _piece_transpose(xs):
    assert len(xs) == PIECES
    piece = lax.broadcasted_iota(jnp.int32, xs[0].shape, 1) // SSM_GROUP_CH
    xs = list(xs)
    for d in (4, 2, 1):
        keep = (piece & d) == 0
        nxt = list(xs)
        for a in range(len(xs)):
            if a & d:
                continue
            b = a + d
            nxt[a] = jnp.where(keep, xs[a], pltpu.roll(xs[b], d * SSM_GROUP_CH, axis=1))
            nxt[b] = jnp.where(keep, pltpu.roll(xs[a], LANES - d * SSM_GROUP_CH, axis=1), xs[b])
        xs = nxt
    return xs


def _inproj_kernel(x_ref, sh_ref, sc_ref, g_ref, w_ref, ug_ref, qkv_ref, u_scr):
    x = x_ref[0]
    h = _rms(x, g_ref[...]) * (1.0 + sc_ref[0]) + sh_ref[0]
    hb = h.astype(BF16)
    u = jnp.dot(hb, w_ref[:, :SSM_WIDTH], preferred_element_type=F32)
    n_chunks = u.shape[0] // SSM_CHUNK
    half = PIECES
    for lb in range(SSM_WIDTH // LANES):
        for c in range(n_chunks):
            u_scr[lb, c * TOK_PITCH:c * TOK_PITCH + SSM_CHUNK, :] = (
                u[c * SSM_CHUNK:(c + 1) * SSM_CHUNK, lb * LANES:(lb + 1) * LANES])
    for lb in range(SSM_WIDTH // LANES):
        for th in range(2):
            xs = [u_scr[lb, pl.ds(th * half + i, n_chunks, stride=TOK_PITCH), :] for i in range(half)]
            ys = _piece_transpose(xs)
            for gl in range(half):
                ug_ref[lb * half + gl, :, th * LANES:(th + 1) * LANES] = ys[gl]
    for j in range(3):
        lo = SSM_WIDTH + j * ATTN_WIDTH
        r = jnp.dot(hb, w_ref[:, lo:lo + ATTN_WIDTH], preferred_element_type=F32)
        if j == 0:
            r = r * (HEAD_DIM ** -0.5)
        qkv_ref[0, :, j * ATTN_WIDTH:(j + 1) * ATTN_WIDTH] = r.astype(BF16)


def _in_proj(x, shift, scale, gain, w_in_bf16, tm=1024):
    nb, seq, _ = x.shape
    vec = pl.BlockSpec((1, 1, D_MODEL), lambda b, i: (b, 0, 0))
    steps = seq // tm
    blk_chunks = tm // SSM_CHUNK
    return pl.pallas_call(
        _inproj_kernel,
        grid=(nb, steps),
        in_specs=[pl.BlockSpec((1, tm, D_MODEL), lambda b, i: (b, i, 0)),
                  vec, vec,
                  pl.BlockSpec((1, D_MODEL), lambda b, i: (0, 0)),
                  pl.BlockSpec((D_MODEL, IN_PROJ_WIDTH), lambda b, i: (0, 0))],
        out_specs=[pl.BlockSpec((N_SSM_GROUPS, blk_chunks, CHUNK_COLS), lambda b, i: (0, b * steps + i, 0)),
                   pl.BlockSpec((1, tm, 3 * ATTN_WIDTH), lambda b, i: (b, i, 0))],
        out_shape=[jax.ShapeDtypeStruct((N_SSM_GROUPS, nb * seq // SSM_CHUNK, CHUNK_COLS), F32),
                   jax.ShapeDtypeStruct((nb, seq, 3 * ATTN_WIDTH), BF16)],
        scratch_shapes=[pltpu.VMEM((SSM_WIDTH // LANES, blk_chunks * TOK_PITCH, LANES), F32)],
        compiler_params=pltpu.CompilerParams(vmem_limit_bytes=VMEM_LIMIT),
        name="in_proj",
    )(x, shift, scale, gain, w_in_bf16)


def _cmul(a, b):
    return a[0] * b[0] - a[1] * b[1], a[0] * b[1] + a[1] * b[0]


def _impulse_kernel(c_ref, w_ref, o_ref):
    for g in range(c_ref.shape[1]):
        o_ref[0, g] = jnp.dot(c_ref[0, g], w_ref[0, g], preferred_element_type=F32,
                              precision=lax.Precision.HIGHEST)


def _impulse_response(c_cat, w_cat, groups_per_step=8):
    n_dir, g_n, h_n, k_n = c_cat.shape
    n_out = w_cat.shape[-1]
    blk = lambda rows, cols: pl.BlockSpec((1, groups_per_step, rows, cols), lambda d, g: (d, g, 0, 0))
    return pl.pallas_call(
        _impulse_kernel,
        grid=(n_dir, g_n // groups_per_step),
        in_specs=[blk(h_n, k_n), blk(k_n, n_out)],
        out_specs=blk(h_n, n_out),
        out_shape=jax.ShapeDtypeStruct((n_dir, g_n, h_n, n_out), F32),
        name="s5_impulse",
    )(c_cat, w_cat)


def _ssm_matrices(a_re, a_im, log_dt, b_re, b_im, c_re, c_im, d_skip):
    hp = lax.Precision.HIGHEST
    c_len, g_n, p_n, h_n = SSM_CHUNK, N_SSM_GROUPS, SSM_STATE, SSM_GROUP_CH
    lam = (jnp.minimum(a_re.astype(F32), -1e-4), a_im.astype(F32))
    dt = jnp.exp(log_dt.astype(F32))[:, :, None]
    z = (lam[0] * dt, lam[1] * dt)
    j = jnp.arange(c_len + 1, dtype=F32)[None, :, None, None]
    mag = jnp.exp(z[0][:, None] * j)
    apow = (mag * jnp.cos(z[1][:, None] * j), mag * jnp.sin(z[1][:, None] * j))
    lam_sq = lam[0] * lam[0] + lam[1] * lam[1]
    zoh = _cmul((apow[0][:, 1] - 1.0, apow[1][:, 1]), (lam[0] / lam_sq, -lam[1] / lam_sq))
    b_bar = _cmul((zoh[0][..., None], zoh[1][..., None]), (b_re.astype(F32), b_im.astype(F32)))
    c = (c_re.astype(F32), c_im.astype(F32))
    c_t = (jnp.swapaxes(c[0], 2, 3), jnp.swapaxes(c[1], 2, 3))
    ap_t = (jnp.transpose(apow[0], (0, 2, 3, 1)), jnp.transpose(apow[1], (0, 2, 3, 1)))
    rep = lambda v: jnp.repeat(v, h_n, axis=-1)
    til = lambda v: jnp.tile(v, (1,) * (v.ndim - 1) + (c_len,))

    w = _cmul((rep(ap_t[0][..., :c_len]), rep(ap_t[1][..., :c_len])), (til(b_bar[0]), til(b_bar[1])))
    kt = _impulse_response(jnp.concatenate([c[0], -c[1]], axis=-1), jnp.concatenate(w, axis=2))
    d_gh = d_skip.astype(F32).reshape(g_n, h_n)
    fwd_rev = kt[0].reshape(g_n, h_n, c_len, h_n)[:, :, :0:-1].reshape(g_n, h_n, (c_len - 1) * h_n)
    lag0 = kt[0][..., :h_n] + kt[1][..., :h_n] + jnp.eye(h_n, dtype=F32)[None] * d_gh[:, :, None]
    lags = jnp.concatenate([fwd_rev, lag0, kt[1][..., h_n:], jnp.zeros((g_n, h_n, h_n), F32)], axis=-1)

    bb_t = (jnp.swapaxes(b_bar[0], 2, 3), jnp.swapaxes(b_bar[1], 2, 3))
    as_t = (jnp.transpose(apow[0], (0, 2, 1, 3)), jnp.transpose(apow[1], (0, 2, 1, 3)))
    pw = tuple(jnp.concatenate([v[0, :, c_len - 1::-1], v[1, :, :c_len]], axis=-1)[:, :, None] for v in as_t)
    bb = tuple(jnp.concatenate([v[0], v[1]], axis=-1)[:, None] for v in bb_t)
    bm = jnp.concatenate(_cmul(pw, bb), axis=-1).reshape(g_n, CHUNK_COLS, 4 * p_n)

    c_cat = tuple(jnp.concatenate([v[0], v[1]], axis=1) for v in c_t)
    q_cat = tuple(jnp.concatenate([v[0, :, :, 1:], v[1, :, :, :0:-1]], axis=1) for v in ap_t)
    wq = _cmul((til(c_cat[0]), til(c_cat[1])), (rep(q_cat[0]), rep(q_cat[1])))
    cm = jnp.concatenate([wq[0], -wq[1]], axis=1)

    a16_re = jnp.concatenate([apow[0][0, c_len], apow[0][1, c_len]], axis=-1)[:, None, :]
    a16_im = jnp.concatenate([apow[1][0, c_len], apow[1][1, c_len]], axis=-1)[:, None, :]
    return bm.astype(BF16), cm.astype(BF16), lags, a16_re, a16_im


def _ssm_kernel(*refs, batches, n_chunks):
    n_t = len(batches)
    u_refs, (bm_ref, cm_ref, lags_ref, are_ref, aim_ref) = refs[:n_t], refs[n_t:n_t + 5]
    y_refs, (s_scr, xf_scr, xb_scr) = refs[n_t + 5:2 * n_t + 5], refs[2 * n_t + 5:]
    nb = sum(batches)
    pitch = n_chunks + BATCH_PITCH_PAD
    ub = jnp.concatenate([r[0] for r in u_refs], axis=0).astype(BF16)
    s = jnp.dot(ub, bm_ref[0], preferred_element_type=F32)
    for b in range(nb):
        for part in range(2):
            s_scr[part, b * pitch:b * pitch + n_chunks, :] = (
                s[b * n_chunks:(b + 1) * n_chunks, part * LANES:(part + 1) * LANES])
    a_re = jnp.broadcast_to(are_ref[0], (nb, LANES))
    a_im = jnp.broadcast_to(aim_ref[0], (nb, LANES))
    is_fwd = lax.broadcasted_iota(jnp.int32, (nb, LANES), 1) < SSM_STATE

    def step(i, carry):
        x_re, x_im = carry
        rf = pl.ds(i, nb, stride=pitch)
        rb = pl.ds(n_chunks - 1 - i, nb, stride=pitch)
        xf_scr[0, rf, :] = x_re
        xf_scr[1, rf, :] = x_im
        xb_scr[0, rb, :] = x_re
        xb_scr[1, rb, :] = x_im
        s_re = jnp.where(is_fwd, s_scr[0, rf, :], s_scr[0, rb, :])
        s_im = jnp.where(is_fwd, s_scr[1, rf, :], s_scr[1, rb, :])
        n_re = a_re * x_re - a_im * x_im + s_re
        n_im = a_re * x_im + a_im * x_re + s_im
        return n_re, n_im

    zero = jnp.zeros((nb, LANES), F32)
    lax.fori_loop(0, n_chunks, step, (zero, zero), unroll=8)
    fwd_rows = lax.broadcasted_iota(jnp.int32, (n_chunks, LANES), 1) < SSM_STATE
    x_in = []
    for b in range(nb):
        rows = slice(b * pitch, b * pitch + n_chunks)
        x_in.append(jnp.concatenate(
            [jnp.where(fwd_rows, xf_scr[part, rows, :], xb_scr[part, rows, :]) for part in range(2)], axis=1))
    x_in = jnp.concatenate(x_in, axis=0).astype(BF16)
    lags = lags_ref[0]
    toep_t = jnp.concatenate(
        [lags[:, (SSM_CHUNK - 1 - t) * SSM_GROUP_CH:(SSM_CHUNK - 1 - t) * SSM_GROUP_CH + CHUNK_COLS]
         for t in range(SSM_CHUNK)], axis=0).astype(BF16)
    y = lax.dot_general(ub, toep_t, (((1,), (1,)), ((), ())), preferred_element_type=F32)
    y = y + jnp.dot(x_in, cm_ref[0], preferred_element_type=F32)
    row0 = 0
    for y_ref, b_t in zip(y_refs, batches):
        y_ref[0] = y[row0:row0 + b_t * n_chunks]
        row0 += b_t * n_chunks


def _ssm(u_gs, bm, cm, lags, a16_re, a16_im, batches):
    g_n = u_gs[0].shape[0]
    n_chunks = u_gs[0].shape[1] // batches[0]
    nb = sum(batches)
    per_g = lambda g: (g, 0, 0)
    rows_spec = [pl.BlockSpec((1, u.shape[1], CHUNK_COLS), per_g) for u in u_gs]
    state_scr = pltpu.VMEM((2, nb * (n_chunks + BATCH_PITCH_PAD), LANES), F32)
    return pl.pallas_call(
        functools.partial(_ssm_kernel, batches=tuple(batches), n_chunks=n_chunks),
        grid=(g_n,),
        in_specs=rows_spec + [pl.BlockSpec((1, CHUNK_COLS, 4 * SSM_STATE), per_g),
                              pl.BlockSpec((1, 4 * SSM_STATE, CHUNK_COLS), per_g),
                              pl.BlockSpec((1, SSM_GROUP_CH, 2 * CHUNK_COLS), per_g),
                              pl.BlockSpec((1, 1, LANES), per_g),
                              pl.BlockSpec((1, 1, LANES), per_g)],
        out_specs=rows_spec,
        out_shape=[jax.ShapeDtypeStruct(u.shape, F32) for u in u_gs],
        scratch_shapes=[state_scr, state_scr, state_scr],
        compiler_params=pltpu.CompilerParams(vmem_limit_bytes=VMEM_LIMIT),
        name="ssm_chunked",
    )(*u_gs, bm, cm, lags, a16_re, a16_im)


def _attn_bias_rows(rpb):
    cols = jnp.arange(GRID_W)
    col_start = jnp.clip(cols - NA_WIN_COLS // 2, 0, GRID_W - NA_WIN_COLS)
    kc = jnp.arange(GRID_W)
    valid = (kc[None, :] >= col_start[:, None]) & (kc[None, :] < col_start[:, None] + NA_WIN_COLS)
    dc = kc[None, :] - cols[:, None] + NA_WIN_COLS - 1
    n_dr, n_dc = 2 * NA_WIN_ROWS - 1, 2 * NA_WIN_COLS - 1
    rpb_ext = jnp.concatenate([rpb.astype(F32), jnp.full(rpb.shape[:2] + (1,), MASK_VALUE, F32)], axis=-1)
    rpb_ext = rpb_ext.reshape(N_HEAD_BLOCKS, HEADS_PER_BLOCK, n_dr, n_dc + 1)
    sel_c = (jnp.where(valid, dc, n_dc)[:, :, None] == jnp.arange(n_dc + 1)).astype(F32)
    by_col = jnp.einsum("bhrd,ckd->brhck", rpb_ext, sel_c, precision=lax.Precision.HIGHEST)
    return by_col.reshape(N_HEAD_BLOCKS, n_dr, HEADS_PER_BLOCK * GRID_W, GRID_W)


def _attn_kernel(q_ref, k_ref, v_ref, bias_ref, o_ref, tab_scr, *, rows_per_step, n_rows):
    wh = min(NA_WIN_ROWS, n_rows)

    @pl.when((pl.program_id(1) == 0) & (pl.program_id(2) == 0))
    def _():
        for case in range(wh):
            for pair in range(wh // 2):
                dr = NA_WIN_ROWS - 1 - case + 2 * pair
                tab_scr[case, :, pair * LANES:(pair + 1) * LANES] = jnp.concatenate(
                    [bias_ref[0, dr], bias_ref[0, dr + 1]], axis=1)

    lane = lax.broadcasted_iota(jnp.int32, (GRID_W, LANES), 1)
    first_head = lane < HEAD_DIM
    zero = jnp.zeros((GRID_W, LANES), BF16)
    ones = jnp.ones((wh * GRID_W, LANES), BF16)
    kv_rows, scores, probs = {}, {}, {}

    def score_phase(j):
        r = pl.program_id(2) * rows_per_step + j
        rs = jnp.clip(r - wh // 2, 0, n_rows - wh)
        q = q_ref[0, pl.ds(pl.multiple_of(r * GRID_W, GRID_W), GRID_W), :]
        qs = jnp.concatenate([jnp.where(first_head, q, zero), jnp.where(first_head, zero, q)], axis=0)
        kv_rows[j] = pl.ds(pl.multiple_of(rs * GRID_W, GRID_W), wh * GRID_W)
        kb = k_ref[0, kv_rows[j], :]
        s = lax.dot_general(qs, kb, (((1,), (1,)), ((), ())), preferred_element_type=F32)
        scores[j] = s + tab_scr[r - rs]

    def softmax_phase(j):
        s = scores.pop(j)
        probs[j] = jnp.exp(s - jnp.max(s, axis=-1, keepdims=True)).astype(BF16)

    def value_phase(j):
        v_aug = jnp.concatenate([v_ref[0, kv_rows[j], :], ones], axis=1)
        o = jnp.dot(probs.pop(j), v_aug, preferred_element_type=F32)
        o = o[:, :LANES] / o[:, LANES:]
        o_ref[0, j * GRID_W:(j + 1) * GRID_W, :] = jnp.where(first_head, o[:GRID_W], o[GRID_W:]).astype(BF16)

    n_groups = rows_per_step // ATTN_GROUP
    for g in range(n_groups + ATTN_LAGS[-1]):
        for phase, lag in zip((score_phase, softmax_phase, value_phase), ATTN_LAGS):
            if 0 <= g - lag < n_groups:
                for j in range((g - lag) * ATTN_GROUP, (g - lag + 1) * ATTN_GROUP):
                    phase(j)


def _attention(qkv, bias_rows, rows_per_step=32):
    nb, seq, _ = qkv.shape
    n_rows = seq // GRID_W
    wh = min(NA_WIN_ROWS, n_rows)
    assert wh == NA_WIN_ROWS and n_rows % rows_per_step == 0
    nhb = N_HEAD_BLOCKS
    return pl.pallas_call(
        functools.partial(_attn_kernel, rows_per_step=rows_per_step, n_rows=n_rows),
        grid=(nhb, nb, n_rows // rows_per_step),
        in_specs=[pl.BlockSpec((1, seq, LANES), lambda h, b, r: (b, 0, h)),
                  pl.BlockSpec((1, seq, LANES), lambda h, b, r: (b, 0, nhb + h)),
                  pl.BlockSpec((1, seq, LANES), lambda h, b, r: (b, 0, 2 * nhb + h)),
                  pl.BlockSpec((1, 2 * NA_WIN_ROWS - 1, HEADS_PER_BLOCK * GRID_W, GRID_W),
                               lambda h, b, r: (h, 0, 0, 0))],
        out_specs=pl.BlockSpec((1, rows_per_step * GRID_W, LANES), lambda h, b, r: (b, r, h)),
        out_shape=jax.ShapeDtypeStruct((nb, seq, ATTN_WIDTH), BF16),
        scratch_shapes=[pltpu.VMEM((wh, HEADS_PER_BLOCK * GRID_W, wh * GRID_W), F32)],
        compiler_params=pltpu.CompilerParams(vmem_limit_bytes=VMEM_LIMIT),
        name="nbr_attention",
    )(qkv, qkv, qkv, bias_rows)


def _gelu_tanh(x):
    return 0.5 * x * (1.0 + jnp.tanh(math.sqrt(2.0 / math.pi) * (x + 0.044715 * (x * x * x))))


def _mix_ffn_kernel(x_ref, yg_ref, ya_ref, gm_ref, sh_ref, sc_ref, gf_ref, wglu_ref, bglu_ref, ns_ref, na_ref,
                    wout_ref, nf_ref, wg_ref, wu_ref, wd_ref, nfin_ref, o_ref, y_scr, x1_scr):
    n_chunks = yg_ref.shape[1]
    for lb in range(SSM_WIDTH // LANES):
        for th in range(2):
            ys = [yg_ref[lb * PIECES + gl, :, th * LANES:(th + 1) * LANES] for gl in range(PIECES)]
            xs = _piece_transpose(ys)
            for i in range(PIECES):
                y_scr[lb, pl.ds(th * PIECES + i, n_chunks, stride=TOK_PITCH), :] = xs[i]
    sub_chunks = n_chunks // MIX_SUBTILES
    sub = sub_chunks * SSM_CHUNK
    for k in range(MIX_SUBTILES):
        rows = slice(k * sub, (k + 1) * sub)
        y_tok = jnp.concatenate(
            [jnp.concatenate([y_scr[lb, c * TOK_PITCH:c * TOK_PITCH + SSM_CHUNK, :]
                              for c in range(k * sub_chunks, (k + 1) * sub_chunks)], axis=0)
             for lb in range(SSM_WIDTH // LANES)], axis=1)
        ys = _gelu_tanh(y_tok)
        z = jnp.dot(ys.astype(BF16), wglu_ref[...], preferred_element_type=F32) + bglu_ref[...]
        ys = ys * jax.nn.sigmoid(z)
        m_s = _rms(ys, ns_ref[...]).astype(BF16)
        m_a = _rms(ya_ref[0, rows, :].astype(F32), na_ref[...]).astype(BF16)
        o = jnp.dot(m_s, wout_ref[:SSM_WIDTH, :], preferred_element_type=F32)
        o = o + jnp.dot(m_a, wout_ref[SSM_WIDTH:, :], preferred_element_type=F32)
        x1_scr[rows, :] = x_ref[0, rows, :] + gm_ref[0] * o
    for t in range(x1_scr.shape[0] // FFN_TILE):
        rows = slice(t * FFN_TILE, (t + 1) * FFN_TILE)
        x1 = x1_scr[rows, :]
        h = (_rms(x1, nf_ref[...]) * (1.0 + sc_ref[0]) + sh_ref[0]).astype(BF16)
        f = jnp.zeros(x1.shape, F32)
        for j in range(D_FF // FF_CHUNK):
            cols = slice(j * FF_CHUNK, (j + 1) * FF_CHUNK)
            gate = jnp.dot(h, wg_ref[:, cols], preferred_element_type=F32)
            up = jnp.dot(h, wu_ref[:, cols], preferred_element_type=F32)
            act = (gate * jax.nn.sigmoid(gate) * up).astype(BF16)
            f = f + jnp.dot(act, wd_ref[cols, :], preferred_element_type=F32)
        o_ref[0, rows, :] = _rms(x1 + gf_ref[0] * f, nfin_ref[...])


def _mix_ffn(x, y_g, y_att, g_mix, sh_ffn, sc_ffn, g_ffn, p, tm=1024):
    nb, seq, _ = x.shape
    steps = seq // tm
    blk_chunks = tm // SSM_CHUNK
    tok = lambda w: pl.BlockSpec((1, tm, w), lambda b, i: (b, i, 0))
    vec = pl.BlockSpec((1, 1, D_MODEL), lambda b, i: (b, 0, 0))
    const = lambda shape: pl.BlockSpec(shape, lambda b, i: (0,) * len(shape), pipeline_mode=pl.Buffered(1))
    return pl.pallas_call(
        _mix_ffn_kernel,
        grid=(nb, steps),
        in_specs=[tok(D_MODEL),
                  pl.BlockSpec((N_SSM_GROUPS, blk_chunks, CHUNK_COLS), lambda b, i: (0, b * steps + i, 0)),
                  tok(ATTN_WIDTH), vec, vec, vec, vec,
                  const((SSM_WIDTH, SSM_WIDTH)), const((1, SSM_WIDTH)),
                  const((1, SSM_WIDTH)), const((1, ATTN_WIDTH)), const((D_MODEL, D_MODEL)),
                  const((1, D_MODEL)), const((D_MODEL, D_FF)), const((D_MODEL, D_FF)), const((D_FF, D_MODEL)),
                  const((1, D_MODEL))],
        out_specs=tok(D_MODEL),
        out_shape=jax.ShapeDtypeStruct((nb, seq, D_MODEL), F32),
        scratch_shapes=[pltpu.VMEM((SSM_WIDTH // LANES, blk_chunks * TOK_PITCH, LANES), F32),
                        pltpu.VMEM((tm, D_MODEL), F32)],
        compiler_params=pltpu.CompilerParams(vmem_limit_bytes=VMEM_LIMIT),
        name="mix_ffn",
    )(x, y_g, y_att, g_mix, sh_ffn, sc_ffn, g_ffn, p["w_glu"], p["b_glu"], p["norm_ssm_out"], p["norm_attn_out"],
      p["w_out"], p["norm_ffn"], p["w_ffn_gate"], p["w_ffn_up"], p["w_ffn_down"], p["norm_final"])


def _trunks(xs, mods, p):
    vecs, u_gs, qkvs = [], [], []
    for x, mod in zip(xs, mods):
        nb = x.shape[0]
        vecs.append([mod[:, i * D_MODEL:(i + 1) * D_MODEL].reshape(nb, 1, D_MODEL) for i in range(N_MOD)])
        sh_mix, sc_mix = vecs[-1][0], vecs[-1][1]
        u_g, qkv = _in_proj(x, sh_mix, sc_mix, p["norm_mix"], p["w_in"])
        u_gs.append(u_g)
        qkvs.append(qkv)
    y_gs = _ssm(u_gs, p["bm"], p["cm"], p["lags"], p["a16_re"], p["a16_im"], [x.shape[0] for x in xs])
    outs = []
    for x, vec, y_g, qkv in zip(xs, vecs, y_gs, qkvs):
        _, _, g_mix, sh_ffn, sc_ffn, g_ffn = vec
        y_att = _attention(qkv, p["bias_rows"])
        outs.append(_mix_ffn(x, y_g, y_att, g_mix, sh_ffn, sc_ffn, g_ffn, p))
    return tuple(outs)


def kernel(x_prompt, x_sample, c_prompt, c_sample, w_ada, b_ada, norm_mix, w_in, ssm_a_re, ssm_a_im, ssm_log_dt, ssm_b_re, ssm_b_im, ssm_c_re, ssm_c_im, ssm_d, w_glu, b_glu, norm_ssm_out, na_rpb, norm_attn_out, w_out, norm_ffn, w_ffn_gate, w_ffn_up, w_ffn_down, norm_final):
    assert w_ada.shape[0] == 1, "single-layer trunk"
    row = lambda v: v.reshape(1, -1).astype(F32)
    bm, cm, lags, a16_re, a16_im = _ssm_matrices(ssm_a_re[0], ssm_a_im[0], ssm_log_dt[0], ssm_b_re[0],
                                                 ssm_b_im[0], ssm_c_re[0], ssm_c_im[0], ssm_d[0])
    p = dict(
        norm_mix=row(norm_mix[0]), w_in=w_in[0].astype(BF16),
        bm=bm, cm=cm, lags=lags, a16_re=a16_re, a16_im=a16_im,
        bias_rows=_attn_bias_rows(na_rpb[0]),
        w_glu=w_glu[0].astype(BF16), b_glu=row(b_glu[0]),
        norm_ssm_out=row(norm_ssm_out[0]), norm_attn_out=row(norm_attn_out[0]),
        w_out=w_out[0].astype(BF16), norm_ffn=row(norm_ffn[0]),
        w_ffn_gate=w_ffn_gate[0].astype(BF16), w_ffn_up=w_ffn_up[0].astype(BF16),
        w_ffn_down=w_ffn_down[0].astype(BF16), norm_final=row(norm_final),
    )
    n_prompt = x_prompt.shape[0]
    mod = _ada_mod(jnp.concatenate([c_prompt, c_sample], axis=0), w_ada[0], b_ada[0])
    return _trunks((x_prompt, x_sample), (mod[:n_prompt], mod[n_prompt:]), p)
```
